```python
import jax, jax.numpy as jnp
from jax import lax
import numpy as np

D_MODEL = 1024
BATCH = 16
SEQ = 2048
DEPTH = 2

N_MIXERS = 2
N_LAYERS_A = (DEPTH + 1) // 2
N_LAYERS_B = DEPTH // 2
EPS = 1e-6
N_MOD = 6

MLA_HEADS = 16
Q_LORA = 512
KV_LORA = 256
QK_NOPE = 64
QK_ROPE = 32
V_HEAD = 64
ROPE_THETA = 10000.0
Q_BLOCK = 128
MASK_VALUE = -1e30

GDN_HEADS = 8
GDN_DK = 128
GDN_DV = 128
CONV_K = 4
CHUNK = 64

D_FF = 2816
N_EXPERTS = 8
TOP_K = 2
D_FF_EXPERT = 1408

MLA_IN = Q_LORA + KV_LORA + QK_ROPE
GDN_QKV = 2 * GDN_HEADS * GDN_DK + GDN_HEADS * GDN_DV
GDN_IN = GDN_QKV + GDN_HEADS * GDN_DV + 2 * GDN_HEADS

kernel_name = 'hybrid_mla_gdn_moe_adaln'


def rms_norm(x, g):
    xf = x.astype(jnp.float32)
    y = xf * lax.rsqrt(jnp.mean(xf * xf, axis=-1, keepdims=True) + EPS)
    return (y * g.astype(jnp.float32)).astype(x.dtype)


def l2_norm(x):
    xf = x.astype(jnp.float32)
    return xf * lax.rsqrt(jnp.sum(xf * xf, axis=-1, keepdims=True) + EPS)


def apply_rope(x, pos):
    half = QK_ROPE // 2
    inv_freq = ROPE_THETA ** (-jnp.arange(half, dtype=jnp.float32) / half)
    ang = pos.astype(jnp.float32)[:, :, None, None] * inv_freq
    cos, sin = jnp.cos(ang), jnp.sin(ang)
    xf = x.astype(jnp.float32)
    x1, x2 = xf[..., :half], xf[..., half:]
    out = jnp.concatenate([x1 * cos - x2 * sin, x2 * cos + x1 * sin], axis=-1)
    return out.astype(x.dtype)


def mla_mixer(h, pos, w_in, q_norm, w_qb, kv_norm, w_kvb, w_out):
    B, S, _ = h.shape
    H = MLA_HEADS
    proj = h @ w_in
    q_lat, kv_lat, k_rope = jnp.split(proj, [Q_LORA, Q_LORA + KV_LORA], axis=-1)
    q = (rms_norm(q_lat, q_norm) @ w_qb).reshape(B, S, H, QK_NOPE + QK_ROPE)
    q_nope = q[..., :QK_NOPE]
    q_rope = apply_rope(q[..., QK_NOPE:], pos)
    kv = (rms_norm(kv_lat, kv_norm) @ w_kvb).reshape(B, S, H, QK_NOPE + V_HEAD)
    k_nope, v = kv[..., :QK_NOPE], kv[..., QK_NOPE:]
    k_rope = apply_rope(k_rope[:, :, None, :], pos)[:, :, 0, :]
    scale = (QK_NOPE + QK_ROPE) ** -0.5
    n_blk = S // Q_BLOCK
    qn_b = q_nope.reshape(B, n_blk, Q_BLOCK, H, QK_NOPE).transpose(1, 0, 2, 3, 4)
    qr_b = q_rope.reshape(B, n_blk, Q_BLOCK, H, QK_ROPE).transpose(1, 0, 2, 3, 4)
    key_idx = jnp.arange(S)

    def attend(args):
        blk, qn, qr = args
        s = (jnp.einsum('bqhd,bkhd->bhqk', qn, k_nope)
             + jnp.einsum('bqhr,bkr->bhqk', qr, k_rope)).astype(jnp.float32) * scale
        q_idx = blk * Q_BLOCK + jnp.arange(Q_BLOCK)
        causal = key_idx[None, :] <= q_idx[:, None]
        s = jnp.where(causal, s, MASK_VALUE)
        p = jax.nn.softmax(s, axis=-1).astype(v.dtype)
        return jnp.einsum('bhqk,bkhd->bqhd', p, v)

    o = lax.map(attend, (jnp.arange(n_blk), qn_b, qr_b))
    o = o.transpose(1, 0, 2, 3, 4).reshape(B, S, H * V_HEAD)
    return o @ w_out


def causal_conv(x, w):
    C = x.shape[-1]
    return lax.conv_general_dilated(
        x, w[:, None, :].astype(x.dtype), window_strides=(1,),
        padding=[(CONV_K - 1, 0)], dimension_numbers=('NWC', 'WIO', 'NWC'),
        feature_group_count=C)


def chunk_gated_delta(q, k, v, g, beta):
    B, S, H, DK = q.shape
    DV = v.shape[-1]
    N = S // CHUNK

    def to_chunks(t):
        return t.astype(jnp.float32).reshape(B, N, CHUNK, H, -1).transpose(0, 3, 1, 2, 4)

    q, k, v = to_chunks(q), to_chunks(k), to_chunks(v)
    beta = beta.astype(jnp.float32).reshape(B, N, CHUNK, H).transpose(0, 3, 1, 2)
    g = g.astype(jnp.float32).reshape(B, N, CHUNK, H).transpose(0, 3, 1, 2)
    gc = jnp.cumsum(g, axis=-1)
    causal = jnp.tril(jnp.ones((CHUNK, CHUNK), dtype=bool))
    strict = jnp.tril(jnp.ones((CHUNK, CHUNK), dtype=bool), -1)
    diff = gc[..., :, None] - gc[..., None, :]
    decay = jnp.where(causal, jnp.exp(jnp.where(causal, diff, 0.0)), 0.0)
    k_beta = k * beta[..., None]
    a_mat = jnp.where(strict, jnp.einsum('bhnid,bhnjd->bhnij', k_beta, k) * decay, 0.0)
    t_mat = a_mat + jnp.eye(CHUNK, dtype=jnp.float32)
    rhs = jnp.concatenate([v * beta[..., None], k_beta * jnp.exp(gc)[..., None]], axis=-1)
    sol = lax.linalg.triangular_solve(t_mat, rhs, left_side=True, lower=True,
                                      unit_diagonal=True)
    u, w = sol[..., :DV], sol[..., DV:]
    attn_intra = jnp.where(causal, jnp.einsum('bhnid,bhnjd->bhnij', q, k) * decay, 0.0)
    q_dec = q * jnp.exp(gc)[..., None]
    g_last = gc[..., -1]
    k_state = k * jnp.exp(g_last[..., None] - gc)[..., None]
    chunk_decay = jnp.exp(g_last)

    def step(state, xs):
        u_c, w_c, qd_c, at_c, ks_c, dec_c = xs
        v_new = u_c - w_c @ state
        o_c = qd_c @ state + at_c @ v_new
        state = state * dec_c[..., None, None] + jnp.einsum('bhcd,bhce->bhde', ks_c, v_new)
        return state, o_c

    xs = (jnp.moveaxis(u, 2, 0), jnp.moveaxis(w, 2, 0), jnp.moveaxis(q_dec, 2, 0),
          jnp.moveaxis(attn_intra, 2, 0), jnp.moveaxis(k_state, 2, 0),
          jnp.moveaxis(chunk_decay, 2, 0))
    s0 = jnp.zeros((B, H, DK, DV), jnp.float32)
    _, o = lax.scan(step, s0, xs)
    return o.transpose(1, 0, 3, 2, 4).reshape(B, S, H, DV)


def gdn_mixer(h, w_in, conv_w, a_log, dt_bias, out_norm, w_out):
    B, S, _ = h.shape
    H = GDN_HEADS
    proj = h @ w_in
    qkv, z, b, a = jnp.split(proj, [GDN_QKV, GDN_QKV + H * GDN_DV, GDN_QKV + H * GDN_DV + H],
                             axis=-1)
    qkv = jax.nn.silu(causal_conv(qkv, conv_w))
    q, k, v = jnp.split(qkv, [H * GDN_DK, 2 * H * GDN_DK], axis=-1)
    q = l2_norm(q.reshape(B, S, H, GDN_DK)) * (GDN_DK ** -0.5)
    k = l2_norm(k.reshape(B, S, H, GDN_DK))
    v = v.reshape(B, S, H, GDN_DV)
    beta = jax.nn.sigmoid(b.astype(jnp.float32))
    g = -jnp.exp(a_log.astype(jnp.float32)) * jax.nn.softplus(
        a.astype(jnp.float32) + dt_bias.astype(jnp.float32))
    o = chunk_gated_delta(q, k, v, g, beta).astype(h.dtype)
    o = rms_norm(o, out_norm) * jax.nn.silu(z.reshape(B, S, H, GDN_DV))
    return o.reshape(B, S, H * GDN_DV) @ w_out


def swiglu_ffn(h, w_gate_up, w_down):
    gt, up = jnp.split(h @ w_gate_up, 2, axis=-1)
    return (jax.nn.silu(gt) * up) @ w_down


def moe_ffn(h, w_router, w_gate_up, w_down):
    B, S, D = h.shape
    t = h.reshape(B * S, D)
    logits = (t @ w_router).astype(jnp.float32)
    top_val, top_idx = lax.top_k(logits, TOP_K)
    top_w = jax.nn.softmax(top_val, axis=-1)
    combine = jnp.sum(jax.nn.one_hot(top_idx, N_EXPERTS, dtype=jnp.float32)
                      * top_w[..., None], axis=1)
    out = jnp.zeros((B * S, D), jnp.float32)
    for e in range(N_EXPERTS):
        y_e = swiglu_ffn(t, w_gate_up[e], w_down[e])
        out = out + combine[:, e:e + 1] * y_e
    return out.reshape(B, S, D).astype(h.dtype)


def setup_inputs(seed: int = 0) -> dict:
    key = jax.random.key(seed)
    ks = iter(jax.random.split(key, 40))
    f32 = jnp.float32

    def nrm(shape, fan_in, scale=1.0):
        return jax.random.normal(next(ks), shape, f32) * (scale * fan_in ** -0.5)

    def gain(shape):
        return 1.0 + 0.05 * jax.random.normal(next(ks), shape, f32)

    D = D_MODEL
    A, Bn = N_LAYERS_A, N_LAYERS_B
    x = jax.random.normal(next(ks), (BATCH, SEQ, D), f32)
    c = jax.random.normal(next(ks), (BATCH, D), f32)
    positions = jnp.broadcast_to(jnp.arange(SEQ, dtype=jnp.int32)[None, :], (BATCH, SEQ))
    dt = jnp.exp(jax.random.uniform(next(ks), (Bn, GDN_HEADS), f32,
                                    minval=jnp.log(0.001), maxval=jnp.log(0.1)))
    return {
        'x': x,
        'c': c,
        'positions': positions,
        'ada_w': nrm((DEPTH, D, N_MOD * D), D, 0.5),
        'ada_b': 0.02 * jax.random.normal(next(ks), (DEPTH, N_MOD * D), f32),
        'norm_mix': gain((DEPTH, D)),
        'norm_ffn': gain((DEPTH, D)),
        'mla_w_in': nrm((A, D, MLA_IN), D),
        'mla_q_norm': gain((A, Q_LORA)),
        'mla_w_qb': nrm((A, Q_LORA, MLA_HEADS * (QK_NOPE + QK_ROPE)), Q_LORA),
        'mla_kv_norm': gain((A, KV_LORA)),
        'mla_w_kvb': nrm((A, KV_LORA, MLA_HEADS * (QK_NOPE + V_HEAD)), KV_LORA),
        'mla_w_out': nrm((A, MLA_HEADS * V_HEAD, D), MLA_HEADS * V_HEAD),
        'ffn_w_gate_up': nrm((A, D, 2 * D_FF), D),
        'ffn_w_down': nrm((A, D_FF, D), D_FF),
        'gdn_w_in': nrm((Bn, D, GDN_IN), D),
        'gdn_conv_w': nrm((Bn, CONV_K, GDN_QKV), CONV_K),
        'gdn_a_log': jnp.log(jax.random.uniform(next(ks), (Bn, GDN_HEADS), f32,
                                                minval=1.0, maxval=16.0)),
        'gdn_dt_bias': dt + jnp.log(-jnp.expm1(-dt)),
        'gdn_out_norm': gain((Bn, GDN_DV)),
        'gdn_w_out': nrm((Bn, GDN_HEADS * GDN_DV, D), GDN_HEADS * GDN_DV),
        'moe_w_router': nrm((Bn, D, N_EXPERTS), D),
        'moe_w_gate_up': nrm((Bn, N_EXPERTS, D, 2 * D_FF_EXPERT), D),
        'moe_w_down': nrm((Bn, N_EXPERTS, D_FF_EXPERT, D), D_FF_EXPERT),
        'final_norm': gain((D,)),
    }


def reference(x, c, positions, ada_w, ada_b, norm_mix, norm_ffn,
              mla_w_in, mla_q_norm, mla_w_qb, mla_kv_norm, mla_w_kvb, mla_w_out,
              ffn_w_gate_up, ffn_w_down,
              gdn_w_in, gdn_conv_w, gdn_a_log, gdn_dt_bias, gdn_out_norm, gdn_w_out,
              moe_w_router, moe_w_gate_up, moe_w_down, final_norm):
    cond = jax.nn.silu(c)
    for i in range(DEPTH):
        j = i // N_MIXERS
        mod = cond @ ada_w[i] + ada_b[i]
        sh1, sc1, g1, sh2, sc2, g2 = [m[:, None, :] for m in jnp.split(mod, N_MOD, axis=-1)]
        h = rms_norm(x, norm_mix[i]) * (1.0 + sc1) + sh1
        if i % N_MIXERS == 0:
            y = mla_mixer(h, positions, mla_w_in[j], mla_q_norm[j], mla_w_qb[j],
                          mla_kv_norm[j], mla_w_kvb[j], mla_w_out[j])
        else:
            y = gdn_mixer(h, gdn_w_in[j], gdn_conv_w[j], gdn_a_log[j], gdn_dt_bias[j],
                          gdn_out_norm[j], gdn_w_out[j])
        x = x + g1 * y
        h = rms_norm(x, norm_ffn[i]) * (1.0 + sc2) + sh2
        if i % 2 == 0:
            y = swiglu_ffn(h, ffn_w_gate_up[j], ffn_w_down[j])
        else:
            y = moe_ffn(h, moe_w_router[j], moe_w_gate_up[j], moe_w_down[j])
        x = x + g2 * y
    return rms_norm(x, final_norm)
```

```python
import functools

import jax
import jax.numpy as jnp
from jax import lax
from jax.experimental import pallas as pl
from jax.experimental.pallas import tpu as pltpu

F32 = jnp.float32
BF16 = jnp.bfloat16

EPS = 1e-6
N_MOD = 6
LANES = 128
VMEM_LIMIT_BYTES = 56 * 1024 * 1024

MLA_HEADS = 16
Q_LORA = 512
KV_LORA = 256
QK_NOPE = 64
QK_ROPE = 32
V_HEAD = 64
ROPE_THETA = 10000.0
MASK_VALUE = -1e30
GDN_HEADS = 8
GDN_DK = 128
GDN_DV = 128
CONV_K = 4
CHUNK = 64
GDN_GROUP = 4
N_EXPERTS = 8


def _params(*sem):
    return pltpu.CompilerParams(dimension_semantics=sem, vmem_limit_bytes=VMEM_LIMIT_BYTES)


def _resident(shape, index=None):
    index = (0,) * len(shape) if index is None else index
    return pl.BlockSpec(shape, lambda *_: index, pipeline_mode=pl.Buffered(1))


def _split3(a):
    hi = a.astype(BF16)
    r = a - hi.astype(F32)
    mid = r.astype(BF16)
    lo = (r - mid.astype(F32)).astype(BF16)
    return hi, mid, lo


def _dot(a, b):
    return jnp.dot(a, b, preferred_element_type=F32)


def _dot_nt(a, b):
    return lax.dot_general(a, b, (((1,), (1,)), ((), ())), preferred_element_type=F32)


def _dot_tn(a, b):
    return lax.dot_general(a, b, (((0,), (0,)), ((), ())), preferred_element_type=F32)


def _dot_f32(a, b):
    a_hi, a_mid, _ = _split3(a)
    b_hi, b_mid, _ = _split3(b)
    return _dot(a_hi, b_hi) + (_dot(a_hi, b_mid) + _dot(a_mid, b_hi))


def _rms(x):
    return x * lax.rsqrt(jnp.mean(x * x, axis=-1, keepdims=True) + EPS)


def _silu(x):
    return x * jax.nn.sigmoid(x)


def _mod_norm(x, gain, sc, sh):
    return _rms(x) * gain * (1.0 + sc) + sh


def _adaln_kernel(c_ref, w_ref, b_ref, o_ref):
    c = c_ref[...]
    cond = _silu(c)
    o_ref[0] = _dot_f32(cond, w_ref[0]) + b_ref[0]


def _adaln(c, ada_w, ada_b):
    depth, d, n = ada_w.shape
    b = c.shape[0]
    tn = d
    return pl.pallas_call(
        _adaln_kernel,
        grid=(depth, n // tn),
        in_specs=[
            pl.BlockSpec((b, d), lambda l, j: (0, 0)),
            pl.BlockSpec((1, d, tn), lambda l, j: (l, 0, j)),
            pl.BlockSpec((1, 1, tn), lambda l, j: (l, 0, j)),
        ],
        out_specs=pl.BlockSpec((1, b, tn), lambda l, j: (l, 0, j)),
        out_shape=jax.ShapeDtypeStruct((depth, b, n), F32),
        compiler_params=_params("parallel", "parallel"),
        name="adaln",
    )(c, ada_w, ada_b.reshape(depth, 1, n))


def _mod_spec(layer, which, batch, tiles_per_seq, d):
    def index(i, *_):
        return ((layer * batch + i // tiles_per_seq) * N_MOD + which, 0, 0)
    return pl.BlockSpec((None, 1, d), index)


def _mla_proj_kernel(x_ref, pos_ref, gain_ref, sc_ref, sh_ref, win_ref, qn_ref, wqb_ref,
                     kvn_ref, wk_ref, wv_ref, freq_ref, sign_ref, q_out, k_out, v_out, *, scale):
    tm = x_ref.shape[0]
    h = _mod_norm(x_ref[...], gain_ref[...], sc_ref[...], sh_ref[...]).astype(BF16)
    proj = _dot(h, win_ref[...])
    q_lat = proj[:, :Q_LORA]
    kv_lat = proj[:, Q_LORA:Q_LORA + KV_LORA]
    k_rope = proj[:, Q_LORA + KV_LORA:]
    qn = (_rms(q_lat) * qn_ref[...]).astype(BF16)
    kvn = (_rms(kv_lat) * kvn_ref[...]).astype(BF16)

    ang = pos_ref[...] * freq_ref[...]
    cos = jnp.cos(ang)
    sin = jnp.sin(ang) * sign_ref[...]
    lane = lax.broadcasted_iota(jnp.int32, (tm, LANES), 1)
    first_half = lane < QK_NOPE + QK_ROPE // 2

    def rope(t):
        partner = jnp.where(first_half, pltpu.roll(t, LANES - QK_ROPE // 2, 1),
                            pltpu.roll(t, QK_ROPE // 2, 1))
        return t * cos + partner * sin

    q = _dot(qn, wqb_ref[...])
    k = _dot(kvn, wk_ref[...])
    k_rope = rope(k_rope)
    for hd in range(MLA_HEADS):
        sl = slice(hd * LANES, (hd + 1) * LANES)
        q_out[:, sl] = (rope(q[:, sl]) * scale).astype(BF16)
        k_out[:, sl] = (k[:, sl] + k_rope).astype(BF16)
    v_out[...] = _dot(kvn, wv_ref[...]).astype(BF16)


def _mla_proj(x2, pos, mod3, gain, w_in, q_norm, w_qb, kv_norm, w_kvb, *, layer, batch, seq, tm):
    t, d = x2.shape
    tiles_per_seq = seq // tm
    hq = MLA_HEADS * LANES
    pad_in = jnp.zeros((d, LANES), F32).at[:, QK_NOPE:QK_NOPE + QK_ROPE].set(w_in[:, Q_LORA + KV_LORA:])
    w_in_p = jnp.concatenate([w_in[:, :Q_LORA + KV_LORA], pad_in], axis=1).astype(BF16)
    w_qb_p = jnp.pad(w_qb.reshape(Q_LORA, MLA_HEADS, QK_NOPE + QK_ROPE),
                     ((0, 0), (0, 0), (0, LANES - QK_NOPE - QK_ROPE))).reshape(Q_LORA, hq).astype(BF16)
    w_kv3 = w_kvb.reshape(KV_LORA, MLA_HEADS, QK_NOPE + V_HEAD)
    w_k_p = jnp.pad(w_kv3[:, :, :QK_NOPE], ((0, 0), (0, 0), (0, LANES - QK_NOPE))
                    ).reshape(KV_LORA, hq).astype(BF16)
    w_v = w_kv3[:, :, QK_NOPE:].reshape(KV_LORA, MLA_HEADS * V_HEAD).astype(BF16)
    half = QK_ROPE // 2
    inv_freq = ROPE_THETA ** (-jnp.arange(half, dtype=F32) / half)
    zeros = jnp.zeros((QK_NOPE,), F32)
    tail = jnp.zeros((LANES - QK_NOPE - QK_ROPE,), F32)
    freq = jnp.concatenate([zeros, inv_freq, inv_freq, tail]).reshape(1, LANES)
    sign = jnp.concatenate([zeros, -jnp.ones((half,), F32), jnp.ones((half,), F32), tail]).reshape(1, LANES)
    scale = float(QK_NOPE + QK_ROPE) ** -0.5

    const = _resident
    return pl.pallas_call(
        functools.partial(_mla_proj_kernel, scale=scale),
        grid=(t // tm,),
        in_specs=[
            pl.BlockSpec((tm, d), lambda i: (i, 0)),
            pl.BlockSpec((tm, 1), lambda i: (i, 0)),
            const((1, d)),
            _mod_spec(layer, 1, batch, tiles_per_seq, d),
            _mod_spec(layer, 0, batch, tiles_per_seq, d),
            const(w_in_p.shape), const((1, Q_LORA)), const(w_qb_p.shape),
            const((1, KV_LORA)), const(w_k_p.shape), const(w_v.shape),
            const((1, LANES)), const((1, LANES)),
        ],
        out_specs=[
            pl.BlockSpec((tm, hq), lambda i: (i, 0)),
            pl.BlockSpec((tm, hq), lambda i: (i, 0)),
            pl.BlockSpec((tm, MLA_HEADS * V_HEAD), lambda i: (i, 0)),
        ],
        out_shape=[
            jax.ShapeDtypeStruct((t, hq), BF16),
            jax.ShapeDtypeStruct((t, hq), BF16),
            jax.ShapeDtypeStruct((t, MLA_HEADS * V_HEAD), BF16),
        ],
        compiler_params=_params("parallel"),
        name="mla_proj",
    )(x2, pos, gain.reshape(1, d), mod3, mod3, w_in_p, q_norm.reshape(1, Q_LORA), w_qb_p,
      kv_norm.reshape(1, KV_LORA), w_k_p, w_v, freq, sign)


def _attn_kernel(q_ref, k_ref, v_ref, o_ref, *, tq):
    qi = pl.program_id(2)
    row = lax.broadcasted_iota(jnp.int32, (tq, tq), 0)
    col = lax.broadcasted_iota(jnp.int32, (tq, tq), 1)
    causal = col <= row
    for hh in range(2):
        q = q_ref[0, :, hh * LANES:(hh + 1) * LANES]

        def tile(j, carry, masked):
            m, l, acc = carry
            r0 = pl.multiple_of(j * tq, tq)
            ks = k_ref[0, pl.ds(r0, tq), hh * LANES:(hh + 1) * LANES]
            vs = v_ref[0, pl.ds(r0, tq), hh * V_HEAD:(hh + 1) * V_HEAD]
            s = _dot_nt(q, ks)
            if masked:
                s = jnp.where(causal, s, MASK_VALUE)
            m_new = jnp.maximum(m, jnp.max(s, axis=-1, keepdims=True))
            p = jnp.exp(s - m_new)
            alpha = jnp.exp(m - m_new)
            l = alpha * l + jnp.sum(p, axis=-1, keepdims=True)
            acc = alpha * acc + _dot(p.astype(BF16), vs)
            return m_new, l, acc

        init = (jnp.full((tq, 1), MASK_VALUE, F32), jnp.zeros((tq, 1), F32),
                jnp.zeros((tq, V_HEAD), F32))
        carry = lax.fori_loop(0, qi, lambda j, c: tile(j, c, False), init)
        _, l, acc = tile(qi, carry, True)
        o_ref[0, :, hh * V_HEAD:(hh + 1) * V_HEAD] = (acc / l).astype(BF16)


def _mla_attention(q, k, v, *, batch, seq, tq):
    hq = MLA_HEADS * LANES
    q3 = q.reshape(batch, seq, hq)
    k3 = k.reshape(batch, seq, hq)
    v3 = v.reshape(batch, seq, MLA_HEADS * V_HEAD)
    out = pl.pallas_call(
        functools.partial(_attn_kernel, tq=tq),
        grid=(batch, MLA_HEADS // 2, seq // tq),
        in_specs=[
            pl.BlockSpec((1, tq, 2 * LANES), lambda b, h, i: (b, i, h)),
            pl.BlockSpec((1, seq, 2 * LANES), lambda b, h, i: (b, 0, h)),
            pl.BlockSpec((1, seq, 2 * V_HEAD), lambda b, h, i: (b, 0, h)),
        ],
        out_specs=pl.BlockSpec((1, tq, 2 * V_HEAD), lambda b, h, i: (b, i, h)),
        out_shape=jax.ShapeDtypeStruct((batch, seq, MLA_HEADS * V_HEAD), BF16),
        compiler_params=_params("parallel", "parallel", "parallel"),
        name="mla_attn",
    )(q3, k3, v3)
    return out.reshape(batch * seq, MLA_HEADS * V_HEAD)


def _out_proj_kernel(a_ref, w_ref, x_ref, gate_ref, o_ref):
    o_ref[...] = x_ref[...] + gate_ref[...] * _dot(a_ref[...], w_ref[...])


def _out_proj(a, w, x2, mod3, *, layer, which, batch, seq, tm, name):
    t, d = x2.shape
    kdim = a.shape[1]
    return pl.pallas_call(
        _out_proj_kernel,
        grid=(t // tm,),
        in_specs=[
            pl.BlockSpec((tm, kdim), lambda i: (i, 0)),
            _resident((kdim, d)),
            pl.BlockSpec((tm, d), lambda i: (i, 0)),
            _mod_spec(layer, which, batch, seq // tm, d),
        ],
        out_specs=pl.BlockSpec((tm, d), lambda i: (i, 0)),
        out_shape=jax.ShapeDtypeStruct((t, d), F32),
        compiler_params=_params("parallel"),
        name=name,
    )(a, w.astype(BF16), x2, mod3)


def _swiglu(h, wg_ref, wu_ref, wd_ref, n_chunks):
    f = wd_ref.shape[0]
    tf = f // n_chunks
    acc = None
    for j in range(n_chunks):
        sl = slice(j * tf, (j + 1) * tf)
        g = _dot(h, wg_ref[:, sl])
        u = _dot(h, wu_ref[:, sl])
        part = _dot((_silu(g) * u).astype(BF16), wd_ref[sl, :])
        acc = part if acc is None else acc + part
    return acc


def _ffn_kernel(x_ref, gain_ref, sc_ref, sh_ref, gate_ref, wg_ref, wu_ref, wd_ref, o_ref, *, n_chunks):
    x = x_ref[...]
    h = _mod_norm(x, gain_ref[...], sc_ref[...], sh_ref[...]).astype(BF16)
    o_ref[...] = x + gate_ref[...] * _swiglu(h, wg_ref, wu_ref, wd_ref, n_chunks)


def _ffn(x2, mod3, gain, w_gate_up, w_down, *, layer, batch, seq, tm):
    t, d = x2.shape
    f = w_down.shape[0]
    tiles_per_seq = seq // tm
    wgu = w_gate_up.astype(BF16)
    return pl.pallas_call(
        functools.partial(_ffn_kernel, n_chunks=2),
        grid=(t // tm,),
        in_specs=[
            pl.BlockSpec((tm, d), lambda i: (i, 0)),
            pl.BlockSpec((1, d), lambda i: (0, 0)),
            _mod_spec(layer, 4, batch, tiles_per_seq, d),
            _mod_spec(layer, 3, batch, tiles_per_seq, d),
            _mod_spec(layer, 5, batch, tiles_per_seq, d),
            _resident((d, f)),
            _resident((d, f), (0, 1)),
            _resident((f, d)),
        ],
        out_specs=pl.BlockSpec((tm, d), lambda i: (i, 0)),
        out_shape=jax.ShapeDtypeStruct((t, d), F32),
        compiler_params=_params("parallel"),
        name="ffn",
    )(x2, gain.reshape(1, d), mod3, mod3, mod3, wgu, wgu, w_down.astype(BF16))


def _gdn_proj_kernel(x_ref, gain_ref, sc_ref, sh_ref, w_ref, conv_ref, alog_ref, dtb_ref,
                     q_out, k_out, v_out, z_out, bg_out, ext_ref, *, tiles_per_seq):
    tm = x_ref.shape[0]
    nqk = GDN_HEADS * GDN_DK
    nqkv = 2 * nqk + GDN_HEADS * GDN_DV
    nz = GDN_HEADS * GDN_DV
    i = pl.program_id(0)
    h = _mod_norm(x_ref[...], gain_ref[...], sc_ref[...], sh_ref[...]).astype(BF16)
    proj = _dot(h, w_ref[...])

    @pl.when(i % tiles_per_seq == 0)
    def _():
        ext_ref[0:8, :] = jnp.zeros((8, nqkv), F32)

    @pl.when(i % tiles_per_seq != 0)
    def _():
        ext_ref[0:8, :] = ext_ref[tm:tm + 8, :]

    ext_ref[8:tm + 8, :] = proj[:, :nqkv]
    y = None
    for tap in range(CONV_K):
        off = 8 - (CONV_K - 1) + tap
        term = conv_ref[tap:tap + 1, :] * ext_ref[off:off + tm, :]
        y = term if y is None else y + term
    qkv = _silu(y)

    def l2n(t):
        return t * lax.rsqrt(jnp.sum(t * t, axis=-1, keepdims=True) + EPS)

    for hd in range(GDN_HEADS):
        qs = slice(hd * GDN_DK, (hd + 1) * GDN_DK)
        ks = slice(nqk + hd * GDN_DK, nqk + (hd + 1) * GDN_DK)
        q_out[:, qs] = (l2n(qkv[:, qs]) * (GDN_DK ** -0.5)).astype(BF16)
        k_out[:, qs] = l2n(qkv[:, ks]).astype(BF16)
    v_out[...] = qkv[:, 2 * nqk:].astype(BF16)
    z_out[...] = proj[:, nqkv:nqkv + nz].astype(BF16)

    ba = proj[:, nqkv + nz:]
    lane = lax.broadcasted_iota(jnp.int32, ba.shape, 1) & (LANES - 1)
    beta = jax.nn.sigmoid(ba)
    pre = ba + dtb_ref[...]
    softplus = jnp.maximum(pre, 0.0) + jnp.log(1.0 + jnp.exp(-jnp.abs(pre)))
    g = jnp.where((lane >= GDN_GROUP) & (lane < 2 * GDN_GROUP), -jnp.exp(alog_ref[...]) * softplus, 0.0)
    row = lax.broadcasted_iota(jnp.int32, (tm, tm), 0)
    col = lax.broadcasted_iota(jnp.int32, (tm, tm), 1)
    same_chunk = (row & -CHUNK) == (col & -CHUNK)
    tril = jnp.where(same_chunk & (col <= row), 1.0, 0.0).astype(BF16)
    g_hi, g_mid, g_lo = _split3(g)
    gc = _dot(tril, g_hi) + (_dot(tril, g_mid) + _dot(tril, g_lo))
    bg_out[...] = jnp.where(lane < GDN_GROUP, beta, gc)


def _gdn_proj(x2, mod3, gain, w_in, conv_w, a_log, dt_bias, *, layer, batch, seq, tm):
    t, d = x2.shape
    nqk = GDN_HEADS * GDN_DK
    nqkv = 2 * nqk + GDN_HEADS * GDN_DV
    nz = GDN_HEADS * GDN_DV
    n_groups = GDN_HEADS // GDN_GROUP
    tiles_per_seq = seq // tm
    wb = w_in[:, nqkv + nz:nqkv + nz + GDN_HEADS]
    wa = w_in[:, nqkv + nz + GDN_HEADS:]

    def group_rows(b_part, a_part, dtype):
        rows = b_part.shape[0]
        chunks = []
        for gidx in range(n_groups):
            sl = slice(gidx * GDN_GROUP, (gidx + 1) * GDN_GROUP)
            chunks += [b_part[:, sl], a_part[:, sl], jnp.zeros((rows, LANES - 2 * GDN_GROUP), dtype)]
        return jnp.concatenate(chunks, axis=1)

    w_p = jnp.concatenate([w_in[:, :nqkv + nz], group_rows(wb, wa, F32)], axis=1).astype(BF16)
    zero_h = jnp.zeros((1, GDN_HEADS), F32)
    alog_p = group_rows(zero_h, a_log.reshape(1, GDN_HEADS), F32)
    dtb_p = group_rows(zero_h, dt_bias.reshape(1, GDN_HEADS), F32)
    const = _resident
    row_spec = lambda n: pl.BlockSpec((tm, n), lambda i: (i, 0))
    return pl.pallas_call(
        functools.partial(_gdn_proj_kernel, tiles_per_seq=tiles_per_seq),
        grid=(t // tm,),
        in_specs=[
            row_spec(d), const((1, d)),
            _mod_spec(layer, 1, batch, tiles_per_seq, d),
            _mod_spec(layer, 0, batch, tiles_per_seq, d),
            const(w_p.shape), const((CONV_K, nqkv)),
            const((1, n_groups * LANES)), const((1, n_groups * LANES)),
        ],
        out_specs=[row_spec(nqk), row_spec(nqk), row_spec(nz), row_spec(nz), row_spec(n_groups * LANES)],
        out_shape=[
            jax.ShapeDtypeStruct((t, nqk), BF16),
            jax.ShapeDtypeStruct((t, nqk), BF16),
            jax.ShapeDtypeStruct((t, nz), BF16),
            jax.ShapeDtypeStruct((t, nz), BF16),
            jax.ShapeDtypeStruct((t, n_groups * LANES), F32),
        ],
        scratch_shapes=[pltpu.VMEM((tm + 8, nqkv), F32)],
        compiler_params=_params("arbitrary"),
        name="gdn_proj",
    )(x2, gain.reshape(1, d), mod3, mod3, w_p, conv_w, alog_p, dtb_p)


def _gdn_core_kernel(q_ref, k_ref, v_ref, z_ref, bg_ref, gn_ref, o_ref, state_ref):
    seq = q_ref.shape[1]
    c = CHUNK
    row = lax.broadcasted_iota(jnp.int32, (c, c), 0)
    col = lax.broadcasted_iota(jnp.int32, (c, c), 1)
    causal = col <= row
    strict = col < row
    eye = jnp.where(row == col, 1.0, 0.0).astype(F32)

    def level_mask(s):
        return ((row & -2 * s) == (col & -2 * s)) & ((row & s) != 0) & ((col & s) == 0)

    lane_zero = jnp.where(lax.broadcasted_iota(jnp.int32, (c, LANES), 1) == 0, 1.0, 0.0).astype(BF16)

    state_ref[...] = jnp.zeros(state_ref.shape, F32)
    gn = gn_ref[...]

    def chunk_step(n, _):
        r0 = pl.multiple_of(n * c, c)
        bg = bg_ref[0, pl.ds(r0, c), :]
        for hh in range(GDN_GROUP):
            sl = slice(hh * GDN_DK, (hh + 1) * GDN_DK)
            q = q_ref[0, pl.ds(r0, c), sl].astype(F32)
            k = k_ref[0, pl.ds(r0, c), sl].astype(F32)
            v = v_ref[0, pl.ds(r0, c), sl].astype(F32)
            z = z_ref[0, pl.ds(r0, c), sl].astype(F32)
            beta = bg[:, hh:hh + 1]
            gc = bg[:, GDN_GROUP + hh:GDN_GROUP + hh + 1]
            g_hi, g_mid, g_lo = _split3(jnp.broadcast_to(gc, (c, LANES)))
            gc_row = _dot_nt(lane_zero, g_hi) + (_dot_nt(lane_zero, g_mid) + _dot_nt(lane_zero, g_lo))
            diff = gc - gc_row
            decay = jnp.where(causal, jnp.exp(jnp.where(causal, diff, 0.0)), 0.0)
            kb = k * beta
            kb16 = kb.astype(BF16)
            k16 = k.astype(BF16)
            a_mat = jnp.where(strict, _dot_nt(kb16, k16) * decay, 0.0)
            inv = eye - jnp.where(level_mask(1), a_mat, 0.0)
            s = 2
            while s < c:
                x_s = jnp.where(level_mask(s), a_mat, 0.0)
                inv = inv - _dot_f32(_dot_f32(inv, x_s), inv)
                s *= 2
            eg = jnp.exp(gc)
            u = _dot_f32(inv, v * beta)
            w = _dot_f32(inv, kb * eg)
            attn = jnp.where(causal, _dot_nt(q.astype(BF16), k16) * decay, 0.0)
            g_last = gc[c - 1:c, :]
            k_state = k * jnp.exp(g_last - gc)
            st = state_ref[hh]
            st16 = st.astype(BF16)
            v_new = u - _dot(w.astype(BF16), st16)
            o = _dot((q * eg).astype(BF16), st16) + _dot(attn.astype(BF16), v_new.astype(BF16))
            state_ref[hh] = st * jnp.exp(g_last) + _dot_tn(k_state.astype(BF16), v_new.astype(BF16))
            o_ref[0, pl.ds(r0, c), sl] = (_rms(o) * gn * _silu(z)).astype(BF16)
        return 0

    lax.fori_loop(0, seq // c, chunk_step, 0)


def _gdn_core(q, k, v, z, bg, out_norm, *, batch, seq):
    n_groups = GDN_HEADS // GDN_GROUP
    gw = GDN_GROUP * GDN_DK
    shp = lambda a: a.reshape(batch, seq, a.shape[-1])
    blk = pl.BlockSpec((1, seq, gw), lambda b, g: (b, 0, g))
    out = pl.pallas_call(
        _gdn_core_kernel,
        grid=(batch, n_groups),
        in_specs=[blk, blk, blk, blk,
                  pl.BlockSpec((1, seq, LANES), lambda b, g: (b, 0, g)),
                  pl.BlockSpec((1, GDN_DV), lambda b, g: (0, 0))],
        out_specs=blk,
        out_shape=jax.ShapeDtypeStruct((batch, seq, GDN_HEADS * GDN_DV), BF16),
        scratch_shapes=[pltpu.VMEM((GDN_GROUP, GDN_DK, GDN_DV), F32)],
        compiler_params=_params("parallel", "parallel"),
        name="gdn_core",
    )(shp(q), shp(k), shp(v), shp(z), shp(bg), out_norm.reshape(1, GDN_DV))
    return out.reshape(batch * seq, GDN_HEADS * GDN_DV)


def _moe_kernel(x_ref, gain_ref, sc_ref, sh_ref, gate_ref, wr_ref, wg_ref, wu_ref, wd_ref, fn_ref,
                o_ref, h_ref, comb_ref, acc_ref):
    e = pl.program_id(1)
    tm = x_ref.shape[0]
    lane = lax.broadcasted_iota(jnp.int32, (tm, LANES), 1)

    @pl.when(e == 0)
    def _():
        h = _mod_norm(x_ref[...], gain_ref[...], sc_ref[...], sh_ref[...])
        h_ref[...] = h.astype(BF16)
        h_hi, h_mid, _ = _split3(h)
        logits = _dot(h_hi, wr_ref[0]) + (_dot(h_hi, wr_ref[1]) + _dot(h_mid, wr_ref[0]))
        lane_f = lane.astype(F32)
        logits = jnp.where(lane < N_EXPERTS, logits, -jnp.inf)
        m1 = jnp.max(logits, axis=-1, keepdims=True)
        i1 = jnp.min(jnp.where(logits == m1, lane_f, float(LANES)), axis=-1, keepdims=True)
        rest = jnp.where(lane_f == i1, -jnp.inf, logits)
        m2 = jnp.max(rest, axis=-1, keepdims=True)
        i2 = jnp.min(jnp.where(rest == m2, lane_f, float(LANES)), axis=-1, keepdims=True)
        e2 = jnp.exp(m2 - m1)
        w1 = 1.0 / (1.0 + e2)
        comb_ref[...] = jnp.where(lane_f == i1, w1, jnp.where(lane_f == i2, e2 * w1, 0.0))
        acc_ref[...] = jnp.zeros(acc_ref.shape, F32)

    y = _swiglu(h_ref[...], wg_ref.at[0], wu_ref.at[0], wd_ref.at[0], 1)
    c_e = jnp.sum(jnp.where(lane == e, comb_ref[...], 0.0), axis=-1, keepdims=True)
    acc_ref[...] += c_e * y

    @pl.when(e == N_EXPERTS - 1)
    def _():
        xo = x_ref[...] + gate_ref[...] * acc_ref[...]
        o_ref[...] = _rms(xo) * fn_ref[...]


def _moe(x2, mod3, gain, w_router, w_gate_up, w_down, final_norm, *, layer, batch, seq, tm):
    t, d = x2.shape
    n_e, fe, _ = w_down.shape
    tiles_per_seq = seq // tm
    wr = jnp.pad(w_router, ((0, 0), (0, LANES - n_e)))
    wr_hi = wr.astype(BF16)
    wr_lo = (wr - wr_hi.astype(F32)).astype(BF16)
    wr2 = jnp.stack([wr_hi, wr_lo])
    wgu = w_gate_up.astype(BF16)
    return pl.pallas_call(
        _moe_kernel,
        grid=(t // tm, n_e),
        in_specs=[
            pl.BlockSpec((tm, d), lambda i, e: (i, 0)),
            pl.BlockSpec((1, d), lambda i, e: (0, 0)),
            _mod_spec(layer, 4, batch, tiles_per_seq, d),
            _mod_spec(layer, 3, batch, tiles_per_seq, d),
            _mod_spec(layer, 5, batch, tiles_per_seq, d),
            pl.BlockSpec((2, d, LANES), lambda i, e: (0, 0, 0)),
            pl.BlockSpec((1, d, fe), lambda i, e: (e, 0, 0)),
            pl.BlockSpec((1, d, fe), lambda i, e: (e, 0, 1)),
            pl.BlockSpec((1, fe, d), lambda i, e: (e, 0, 0)),
            pl.BlockSpec((1, d), lambda i, e: (0, 0)),
        ],
        out_specs=pl.BlockSpec((tm, d), lambda i, e: (i, 0)),
        out_shape=jax.ShapeDtypeStruct((t, d), F32),
        scratch_shapes=[pltpu.VMEM((tm, d), BF16), pltpu.VMEM((tm, LANES), F32), pltpu.VMEM((tm, d), F32)],
        compiler_params=_params("parallel", "arbitrary"),
        name="moe",
    )(x2, gain.reshape(1, d), mod3, mod3, mod3, wr2, wgu, wgu, w_down.astype(BF16),
      final_norm.reshape(1, d))


def kernel(x, c, positions, ada_w, ada_b, norm_mix, norm_ffn, mla_w_in, mla_q_norm, mla_w_qb, mla_kv_norm, mla_w_kvb, mla_w_out, ffn_w_gate_up, ffn_w_down, gdn_w_in, gdn_conv_w, gdn_a_log, gdn_dt_bias, gdn_out_norm, gdn_w_out, moe_w_router, moe_w_gate_up, moe_w_down, final_norm):
    batch, seq, d = x.shape
    depth = ada_w.shape[0]
    assert depth == 2 and seq % 512 == 0
    t = batch * seq
    tm = 512
    dims = dict(batch=batch, seq=seq, tm=tm)

    mod = _adaln(c, ada_w, ada_b)
    mod3 = mod.reshape(depth * batch * N_MOD, 1, d)
    x2 = x.reshape(t, d)
    pos = positions.astype(F32).reshape(t, 1)

    q, k, v = _mla_proj(x2, pos, mod3, norm_mix[0], mla_w_in[0], mla_q_norm[0], mla_w_qb[0],
                        mla_kv_norm[0], mla_w_kvb[0], layer=0, **dims)
    attn = _mla_attention(q, k, v, batch=batch, seq=seq, tq=256)
    x2 = _out_proj(attn, mla_w_out[0], x2, mod3, layer=0, which=2, name="mla_out", **dims)
    x2 = _ffn(x2, mod3, norm_ffn[0], ffn_w_gate_up[0], ffn_w_down[0], layer=0, **dims)

    gq, gk, gv, gz, bg = _gdn_proj(x2, mod3, norm_mix[1], gdn_w_in[0], gdn_conv_w[0], gdn_a_log[0],
                                   gdn_dt_bias[0], layer=1, **dims)
    og = _gdn_core(gq, gk, gv, gz, bg, gdn_out_norm[0], batch=batch, seq=seq)
    x2 = _out_proj(og, gdn_w_out[0], x2, mod3, layer=1, which=2, name="gdn_out", **dims)
    out = _moe(x2, mod3, norm_ffn[1], moe_w_router[0], moe_w_gate_up[0], moe_w_down[0], final_norm,
               layer=1, **dims)
    return out.reshape(batch, seq, d)
```

```python
import functools

import jax
import jax.numpy as jnp
from jax import lax
from jax.experimental import pallas as pl
from jax.experimental.pallas import tpu as pltpu

F32 = jnp.float32
BF16 = jnp.bfloat16

EPS = 1e-6
N_MOD = 6
LANES = 128
VMEM_LIMIT_BYTES = 56 * 1024 * 1024

MLA_HEADS = 16
Q_LORA = 512
KV_LORA = 256
QK_NOPE = 64
QK_ROPE = 32
V_HEAD = 64
ROPE_THETA = 10000.0
MASK_VALUE = -1e30
GDN_HEADS = 8
GDN_DK = 128
GDN_DV = 128
CONV_K = 4
CHUNK = 64
GDN_GROUP = GDN_HEADS
N_EXPERTS = 8


def _params(*sem):
    return pltpu.CompilerParams(dimension_semantics=sem, vmem_limit_bytes=VMEM_LIMIT_BYTES)


def _resident(shape, index=None):
    index = (0,) * len(shape) if index is None else index
    return pl.BlockSpec(shape, lambda *_: index, pipeline_mode=pl.Buffered(1))


def _split3(a):
    hi = a.astype(BF16)
    r = a - hi.astype(F32)
    mid = r.astype(BF16)
    lo = (r - mid.astype(F32)).astype(BF16)
    return hi, mid, lo


def _dot(a, b):
    return jnp.dot(a, b, preferred_element_type=F32)


def _dot_nt(a, b):
    return lax.dot_general(a, b, (((1,), (1,)), ((), ())), preferred_element_type=F32)


def _dot_tn(a, b):
    return lax.dot_general(a, b, (((0,), (0,)), ((), ())), preferred_element_type=F32)


def _bmm(a, b):
    return lax.dot_general(a, b, (((2,), (1,)), ((0,), (0,))), preferred_element_type=F32)


def _bmm_nt(a, b):
    return lax.dot_general(a, b, (((2,), (2,)), ((0,), (0,))), preferred_element_type=F32)


def _bmm_tn(a, b):
    return lax.dot_general(a, b, (((1,), (1,)), ((0,), (0,))), preferred_element_type=F32)


def _dot_f32(a, b):
    a_hi, a_mid, _ = _split3(a)
    b_hi, b_mid, _ = _split3(b)
    return _dot(a_hi, b_hi) + (_dot(a_hi, b_mid) + _dot(a_mid, b_hi))


def _rms(x):
    return x * lax.rsqrt(jnp.mean(x * x, axis=-1, keepdims=True) + EPS)


def _silu(x):
    return x * jax.nn.sigmoid(x)


def _mod_norm(x, gain, sc, sh):
    return _rms(x) * gain * (1.0 + sc) + sh


def _adaln_kernel(c_ref, w_ref, b_ref, o_ref):
    c = c_ref[...]
    cond = _silu(c)
    o_ref[0] = _dot_f32(cond, w_ref[0]) + b_ref[0]


def _adaln(c, ada_w, ada_b):
    depth, d, n = ada_w.shape
    b = c.shape[0]
    tn = d
    return pl.pallas_call(
        _adaln_kernel,
        grid=(depth, n // tn),
        in_specs=[
            pl.BlockSpec((b, d), lambda l, j: (0, 0)),
            pl.BlockSpec((1, d, tn), lambda l, j: (l, 0, j)),
            pl.BlockSpec((1, 1, tn), lambda l, j: (l, 0, j)),
        ],
        out_specs=pl.BlockSpec((1, b, tn), lambda l, j: (l, 0, j)),
        out_shape=jax.ShapeDtypeStruct((depth, b, n), F32),
        compiler_params=_params("parallel", "parallel"),
        name="adaln",
    )(c, ada_w, ada_b.reshape(depth, 1, n))


def _mod_spec(layer, which, batch, tiles_per_seq, d):
    def index(i, *_):
        return ((layer * batch + i // tiles_per_seq) * N_MOD + which, 0, 0)
    return pl.BlockSpec((None, 1, d), index)


def _mla_proj_kernel(x_ref, pos_ref, gain_ref, sc_ref, sh_ref, win_ref, qn_ref, wqb_ref,
                     kvn_ref, wk_ref, wv_ref, freq_ref, sign_ref, q_out, k_out, vt_out, *, scale):
    tm = x_ref.shape[0]
    h = _mod_norm(x_ref[...], gain_ref[...], sc_ref[...], sh_ref[...]).astype(BF16)
    proj = _dot(h, win_ref[...])
    q_lat = proj[:, :Q_LORA]
    kv_lat = proj[:, Q_LORA:Q_LORA + KV_LORA]
    k_rope = proj[:, Q_LORA + KV_LORA:]
    qn = (_rms(q_lat) * qn_ref[...]).astype(BF16)
    kvn = (_rms(kv_lat) * kvn_ref[...]).astype(BF16)

    ang = pos_ref[...] * freq_ref[...]
    cos = jnp.cos(ang)
    sin = jnp.sin(ang) * sign_ref[...]
    lane = lax.broadcasted_iota(jnp.int32, (tm, LANES), 1)
    first_half = lane < QK_NOPE + QK_ROPE // 2

    def rope(t):
        partner = jnp.where(first_half, pltpu.roll(t, LANES - QK_ROPE // 2, 1),
                            pltpu.roll(t, QK_ROPE // 2, 1))
        return t * cos + partner * sin

    q = _dot(qn, wqb_ref[...])
    k = _dot(kvn, wk_ref[...])
    k_rope = rope(k_rope)
    for hd in range(MLA_HEADS):
        sl = slice(hd * LANES, (hd + 1) * LANES)
        q_out[:, sl] = (rope(q[:, sl]) * scale).astype(BF16)
        k_out[:, sl] = (k[:, sl] + k_rope).astype(BF16)
    vt = _dot(kvn, wv_ref[...]).T
    tkv = vt_out.shape[2]
    for j in range(vt_out.shape[0]):
        vt_out[j] = vt[:, j * tkv:(j + 1) * tkv].astype(BF16)


def _mla_proj(x2, pos, mod3, gain, w_in, q_norm, w_qb, kv_norm, w_kvb, *, layer, batch, seq, tm, tkv):
    t, d = x2.shape
    tiles_per_seq = seq // tm
    hq = MLA_HEADS * LANES
    pad_in = jnp.zeros((d, LANES), F32).at[:, QK_NOPE:QK_NOPE + QK_ROPE].set(w_in[:, Q_LORA + KV_LORA:])
    w_in_p = jnp.concatenate([w_in[:, :Q_LORA + KV_LORA], pad_in], axis=1).astype(BF16)
    w_qb_p = jnp.pad(w_qb.reshape(Q_LORA, MLA_HEADS, QK_NOPE + QK_ROPE),
                     ((0, 0), (0, 0), (0, LANES - QK_NOPE - QK_ROPE))).reshape(Q_LORA, hq).astype(BF16)
    w_kv3 = w_kvb.reshape(KV_LORA, MLA_HEADS, QK_NOPE + V_HEAD)
    w_k_p = jnp.pad(w_kv3[:, :, :QK_NOPE], ((0, 0), (0, 0), (0, LANES - QK_NOPE))
                    ).reshape(KV_LORA, hq).astype(BF16)
    w_v = w_kv3[:, :, QK_NOPE:].reshape(KV_LORA, MLA_HEADS * V_HEAD).astype(BF16)
    half = QK_ROPE // 2
    inv_freq = ROPE_THETA ** (-jnp.arange(half, dtype=F32) / half)
    zeros = jnp.zeros((QK_NOPE,), F32)
    tail = jnp.zeros((LANES - QK_NOPE - QK_ROPE,), F32)
    freq = jnp.concatenate([zeros, inv_freq, inv_freq, tail]).reshape(1, LANES)
    sign = jnp.concatenate([zeros, -jnp.ones((half,), F32), jnp.ones((half,), F32), tail]).reshape(1, LANES)
    scale = float(QK_NOPE + QK_ROPE) ** -0.5

    const = _resident
    return pl.pallas_call(
        functools.partial(_mla_proj_kernel, scale=scale),
        grid=(t // tm,),
        in_specs=[
            pl.BlockSpec((tm, d), lambda i: (i, 0)),
            pl.BlockSpec((tm, 1), lambda i: (i, 0)),
            const((1, d)),
            _mod_spec(layer, 1, batch, tiles_per_seq, d),
            _mod_spec(layer, 0, batch, tiles_per_seq, d),
            const(w_in_p.shape), const((1, Q_LORA)), const(w_qb_p.shape),
            const((1, KV_LORA)), const(w_k_p.shape), const(w_v.shape),
            const((1, LANES)), const((1, LANES)),
        ],
        out_specs=[
            pl.BlockSpec((tm, hq), lambda i: (i, 0)),
            pl.BlockSpec((tm, hq), lambda i: (i, 0)),
            pl.BlockSpec((tm // tkv, MLA_HEADS * V_HEAD, tkv), lambda i: (i, 0, 0)),
        ],
        out_shape=[
            jax.ShapeDtypeStruct((t, hq), BF16),
            jax.ShapeDtypeStruct((t, hq), BF16),
            jax.ShapeDtypeStruct((t // tkv, MLA_HEADS * V_HEAD, tkv), BF16),
        ],
        compiler_params=_params("parallel"),
        name="mla_proj",
    )(x2, pos, gain.reshape(1, d), mod3, mod3, w_in_p, q_norm.reshape(1, Q_LORA), w_qb_p,
      kv_norm.reshape(1, KV_LORA), w_k_p, w_v, freq, sign)


def _attn_kernel(q_ref, k_ref, vt_ref, o_ref, *, tq, n_heads):
    qi = pl.program_id(2)
    kv_idx = lax.broadcasted_iota(jnp.int32, (tq, tq), 0)
    q_idx = lax.broadcasted_iota(jnp.int32, (tq, tq), 1)
    causal = kv_idx <= q_idx
    heads = range(n_heads)
    qs = [q_ref[0, :, h * LANES:(h + 1) * LANES] for h in heads]

    def tile(j, carry, masked):
        r0 = pl.multiple_of(j * tq, tq)
        s = [_dot_nt(k_ref[0, pl.ds(r0, tq), h * LANES:(h + 1) * LANES], qs[h]) for h in heads]
        if masked:
            s = [jnp.where(causal, x, MASK_VALUE) for x in s]
        m_new = [jnp.maximum(carry[h][0], jnp.max(s[h], axis=0, keepdims=True)) for h in heads]
        p = [jnp.exp(s[h] - m_new[h]) for h in heads]
        alpha = [jnp.exp(carry[h][0] - m_new[h]) for h in heads]
        l_new = [alpha[h] * carry[h][1] + jnp.sum(p[h], axis=0, keepdims=True) for h in heads]
        pv = [_dot(vt_ref[j, h * V_HEAD:(h + 1) * V_HEAD, :], p[h].astype(BF16)) for h in heads]
        return tuple((m_new[h], l_new[h], alpha[h] * carry[h][2] + pv[h]) for h in heads)

    init = tuple((jnp.full((1, tq), MASK_VALUE, F32), jnp.zeros((1, tq), F32),
                  jnp.zeros((V_HEAD, tq), F32)) for _ in heads)
    carry = lax.fori_loop(0, qi, lambda j, c: tile(j, c, False), init)
    carry = tile(qi, carry, True)
    for pair in range(n_heads // 2):
        o_t = jnp.concatenate([carry[h][2] / carry[h][1] for h in (2 * pair, 2 * pair + 1)], axis=0)
        o_ref[0, :, pair * LANES:(pair + 1) * LANES] = o_t.T.astype(BF16)


def _mla_attention(q, k, vt, *, batch, seq, tq, n_heads):
    hq = MLA_HEADS * LANES
    q3 = q.reshape(batch, seq, hq)
    k3 = k.reshape(batch, seq, hq)
    n_kv = seq // tq
    out = pl.pallas_call(
        functools.partial(_attn_kernel, tq=tq, n_heads=n_heads),
        grid=(batch, MLA_HEADS // n_heads, seq // tq),
        in_specs=[
            pl.BlockSpec((1, tq, n_heads * LANES), lambda b, h, i: (b, i, h)),
            pl.BlockSpec((1, seq, n_heads * LANES), lambda b, h, i: (b, 0, h)),
            pl.BlockSpec((n_kv, n_heads * V_HEAD, tq), lambda b, h, i: (b, h, 0)),
        ],
        out_specs=pl.BlockSpec((1, tq, n_heads * V_HEAD), lambda b, h, i: (b, i, h)),
        out_shape=jax.ShapeDtypeStruct((batch, seq, MLA_HEADS * V_HEAD), BF16),
        compiler_params=_params("parallel", "parallel", "parallel"),
        name="mla_attn",
    )(q3, k3, vt)
    return out.reshape(batch * seq, MLA_HEADS * V_HEAD)


def _out_proj_kernel(a_ref, w_ref, x_ref, gate_ref, o_ref):
    o_ref[...] = x_ref[...] + gate_ref[...] * _dot(a_ref[...], w_ref[...])


def _out_proj(a, w, x2, mod3, *, layer, which, batch, seq, tm, name):
    t, d = x2.shape
    kdim = a.shape[1]
    return pl.pallas_call(
        _out_proj_kernel,
        grid=(t // tm,),
        in_specs=[
            pl.BlockSpec((tm, kdim), lambda i: (i, 0)),
            _resident((kdim, d)),
            pl.BlockSpec((tm, d), lambda i: (i, 0)),
            _mod_spec(layer, which, batch, seq // tm, d),
        ],
        out_specs=pl.BlockSpec((tm, d), lambda i: (i, 0)),
        out_shape=jax.ShapeDtypeStruct((t, d), F32),
        compiler_params=_params("parallel"),
        name=name,
    )(a, w.astype(BF16), x2, mod3)


def _swiglu(h, wg_ref, wu_ref, wd_ref, n_chunks):
    f = wd_ref.shape[0]
    tf = f // n_chunks
    acc = None
    for j in range(n_chunks):
        sl = slice(j * tf, (j + 1) * tf)
        g = _dot(h, wg_ref[:, sl])
        u = _dot(h, wu_ref[:, sl])
        part = _dot((_silu(g) * u).astype(BF16), wd_ref[sl, :])
        acc = part if acc is None else acc + part
    return acc


def _ffn_kernel(x_ref, gain_ref, sc_ref, sh_ref, gate_ref, wg_ref, wu_ref, wd_ref, o_ref, *, n_chunks):
    x = x_ref[...]
    h = _mod_norm(x, gain_ref[...], sc_ref[...], sh_ref[...]).astype(BF16)
    o_ref[...] = x + gate_ref[...] * _swiglu(h, wg_ref, wu_ref, wd_ref, n_chunks)


def _ffn(x2, mod3, gain, w_gate_up, w_down, *, layer, batch, seq, tm):
    t, d = x2.shape
    f = w_down.shape[0]
    tiles_per_seq = seq // tm
    wgu = w_gate_up.astype(BF16)
    return pl.pallas_call(
        functools.partial(_ffn_kernel, n_chunks=2),
        grid=(t // tm,),
        in_specs=[
            pl.BlockSpec((tm, d), lambda i: (i, 0)),
            pl.BlockSpec((1, d), lambda i: (0, 0)),
            _mod_spec(layer, 4, batch, tiles_per_seq, d),
            _mod_spec(layer, 3, batch, tiles_per_seq, d),
            _mod_spec(layer, 5, batch, tiles_per_seq, d),
            _resident((d, f)),
            _resident((d, f), (0, 1)),
            _resident((f, d)),
        ],
        out_specs=pl.BlockSpec((tm, d), lambda i: (i, 0)),
        out_shape=jax.ShapeDtypeStruct((t, d), F32),
        compiler_params=_params("parallel"),
        name="ffn",
    )(x2, gain.reshape(1, d), mod3, mod3, mod3, wgu, wgu, w_down.astype(BF16))


def _gdn_proj_kernel(x_ref, gain_ref, sc_ref, sh_ref, w_ref, conv_ref, alog_ref, dtb_ref,
                     q_out, k_out, v_out, z_out, bg_out, ext_ref, *, tiles_per_seq):
    tm = x_ref.shape[0]
    nqk = GDN_HEADS * GDN_DK
    nqkv = 2 * nqk + GDN_HEADS * GDN_DV
    nz = GDN_HEADS * GDN_DV
    i = pl.program_id(0)
    h = _mod_norm(x_ref[...], gain_ref[...], sc_ref[...], sh_ref[...]).astype(BF16)
    proj = _dot(h, w_ref[...])

    @pl.when(i % tiles_per_seq == 0)
    def _():
        ext_ref[0:8, :] = jnp.zeros((8, nqkv), F32)

    @pl.when(i % tiles_per_seq != 0)
    def _():
        ext_ref[0:8, :] = ext_ref[tm:tm + 8, :]

    ext_ref[8:tm + 8, :] = proj[:, :nqkv]
    y = None
    for tap in range(CONV_K):
        off = 8 - (CONV_K - 1) + tap
        term = conv_ref[tap:tap + 1, :] * ext_ref[off:off + tm, :]
        y = term if y is None else y + term
    qkv = _silu(y)

    def l2n(t):
        return t * lax.rsqrt(jnp.sum(t * t, axis=-1, keepdims=True) + EPS)

    for hd in range(GDN_HEADS):
        qs = slice(hd * GDN_DK, (hd + 1) * GDN_DK)
        ks = slice(nqk + hd * GDN_DK, nqk + (hd + 1) * GDN_DK)
        q_out[:, qs] = (l2n(qkv[:, qs]) * (GDN_DK ** -0.5)).astype(BF16)
        k_out[:, qs] = l2n(qkv[:, ks]).astype(BF16)
    v_out[...] = qkv[:, 2 * nqk:].astype(BF16)
    z_out[...] = proj[:, nqkv:nqkv + nz].astype(BF16)

    ba = proj[:, nqkv + nz:]
    lane = lax.broadcasted_iota(jnp.int32, ba.shape, 1) & (LANES - 1)
    beta = jax.nn.sigmoid(ba)
    pre = ba + dtb_ref[...]
    softplus = jnp.maximum(pre, 0.0) + jnp.log(1.0 + jnp.exp(-jnp.abs(pre)))
    g = jnp.where((lane >= GDN_GROUP) & (lane < 2 * GDN_GROUP), -jnp.exp(alog_ref[...]) * softplus, 0.0)
    row = lax.broadcasted_iota(jnp.int32, (tm, tm), 0)
    col = lax.broadcasted_iota(jnp.int32, (tm, tm), 1)
    same_chunk = (row & -CHUNK) == (col & -CHUNK)
    tril = jnp.where(same_chunk & (col <= row), 1.0, 0.0).astype(BF16)
    g_hi, g_mid, g_lo = _split3(g)
    gc = _dot(tril, g_hi) + (_dot(tril, g_mid) + _dot(tril, g_lo))
    bg_out[...] = jnp.where(lane < GDN_GROUP, beta, gc)


def _gdn_proj(x2, mod3, gain, w_in, conv_w, a_log, dt_bias, *, layer, batch, seq, tm):
    t, d = x2.shape
    nqk = GDN_HEADS * GDN_DK
    nqkv = 2 * nqk + GDN_HEADS * GDN_DV
    nz = GDN_HEADS * GDN_DV
    n_groups = GDN_HEADS // GDN_GROUP
    tiles_per_seq = seq // tm
    wb = w_in[:, nqkv + nz:nqkv + nz + GDN_HEADS]
    wa = w_in[:, nqkv + nz + GDN_HEADS:]

    def group_rows(b_part, a_part, dtype):
        rows = b_part.shape[0]
        chunks = []
        for gidx in range(n_groups):
            sl = slice(gidx * GDN_GROUP, (gidx + 1) * GDN_GROUP)
            chunks += [b_part[:, sl], a_part[:, sl], jnp.zeros((rows, LANES - 2 * GDN_GROUP), dtype)]
        return jnp.concatenate(chunks, axis=1)

    w_p = jnp.concatenate([w_in[:, :nqkv + nz], group_rows(wb, wa, F32)], axis=1).astype(BF16)
    zero_h = jnp.zeros((1, GDN_HEADS), F32)
    alog_p = group_rows(zero_h, a_log.reshape(1, GDN_HEADS), F32)
    dtb_p = group_rows(zero_h, dt_bias.reshape(1, GDN_HEADS), F32)
    const = _resident
    row_spec = lambda n: pl.BlockSpec((tm, n), lambda i: (i, 0))
    return pl.pallas_call(
        functools.partial(_gdn_proj_kernel, tiles_per_seq=tiles_per_seq),
        grid=(t // tm,),
        in_specs=[
            row_spec(d), const((1, d)),
            _mod_spec(layer, 1, batch, tiles_per_seq, d),
            _mod_spec(layer, 0, batch, tiles_per_seq, d),
            const(w_p.shape), const((CONV_K, nqkv)),
            const((1, n_groups * LANES)), const((1, n_groups * LANES)),
        ],
        out_specs=[row_spec(nqk), row_spec(nqk), row_spec(nz), row_spec(nz), row_spec(n_groups * LANES)],
        out_shape=[
            jax.ShapeDtypeStruct((t, nqk), BF16),
            jax.ShapeDtypeStruct((t, nqk), BF16),
            jax.ShapeDtypeStruct((t, nz), BF16),
            jax.ShapeDtypeStruct((t, nz), BF16),
            jax.ShapeDtypeStruct((t, n_groups * LANES), F32),
        ],
        scratch_shapes=[pltpu.VMEM((tm + 8, nqkv), F32)],
        compiler_params=_params("arbitrary"),
        name="gdn_proj",
    )(x2, gain.reshape(1, d), mod3, mod3, w_p, conv_w, alog_p, dtb_p)


def _gdn_core_kernel(q_ref, k_ref, v_ref, z_ref, bg_ref, gn_ref, o_ref, state_ref):
    ts = q_ref.shape[1]
    c = CHUNK
    nc = ts // c
    nh = GDN_HEADS
    row = lax.broadcasted_iota(jnp.int32, (1, c, c), 1)
    col = lax.broadcasted_iota(jnp.int32, (1, c, c), 2)
    causal = col <= row
    strict = col < row
    eye = jnp.where(row == col, 1.0, 0.0).astype(F32)

    def level_mask(s):
        return ((row & -2 * s) == (col & -2 * s)) & ((row & s) != 0) & ((col & s) == 0)

    lane = lax.broadcasted_iota(jnp.int32, (nh * nc, c, LANES), 2)

    @pl.when(pl.program_id(1) == 0)
    def _():
        state_ref[...] = jnp.zeros(state_ref.shape, F32)

    bg = bg_ref[0]

    def stack(ref):
        return jnp.concatenate(
            [ref[0, :, hd * GDN_DK:(hd + 1) * GDN_DK].reshape(nc, c, GDN_DK) for hd in range(nh)], axis=0)

    def stack_col(first_lane):
        return jnp.concatenate(
            [bg[:, first_lane + hd:first_lane + hd + 1].reshape(nc, c, 1) for hd in range(nh)], axis=0)

    q = stack(q_ref).astype(F32)
    k16 = stack(k_ref)
    k = k16.astype(F32)
    v = stack(v_ref).astype(F32)
    beta = stack_col(0)
    gc = stack_col(nh)

    g_hi, g_mid, g_lo = (t.astype(F32) for t in _split3(gc))
    terms = lambda base, other: jnp.where(lane == base, g_hi, jnp.where(
        lane == base + 1, g_mid, jnp.where(lane == base + 2, g_lo, other)))
    lhs = terms(0, jnp.where(lane < 6, 1.0, 0.0)).astype(BF16)
    rhs = terms(3, jnp.where(lane < 3, -1.0, 0.0)).astype(BF16)
    diff = -_bmm_nt(lhs, rhs)
    decay = jnp.where(causal, jnp.exp(jnp.where(causal, diff, 0.0)), 0.0)

    kb = k * beta
    a_mat = jnp.where(strict, _bmm_nt(kb.astype(BF16), k16) * decay, 0.0)
    attn = jnp.where(causal, _bmm_nt(q.astype(BF16), k16) * decay, 0.0).astype(BF16)

    inv = eye - jnp.where(level_mask(1), a_mat, 0.0)
    s = 2
    while s < c:
        x_s = jnp.where(level_mask(s), a_mat, 0.0).astype(BF16)
        inv16 = inv.astype(BF16)
        inv = inv - _bmm(_bmm(inv16, x_s).astype(BF16), inv16)
        s *= 2

    eg = jnp.exp(gc)
    g_last = gc[:, c - 1:c, :]
    rhs_uw = jnp.concatenate([v * beta, kb * eg], axis=-1).astype(BF16)
    uw = _bmm(inv.astype(BF16), rhs_uw)
    a_uw = _bmm(attn, uw.astype(BF16))
    o_intra = a_uw[..., :GDN_DV]
    q_eff = (q * eg - a_uw[..., GDN_DV:]).astype(BF16)
    kt_uw = _bmm_tn(k16, (uw * jnp.exp(g_last - gc)).astype(BF16))
    s_add = kt_uw[..., :GDN_DV]
    s_mul = kt_uw[..., GDN_DV:].astype(BF16)
    dec = jnp.exp(g_last)

    per_head = lambda a: a.reshape((nh, nc) + a.shape[1:])
    q_eff, o_intra, s_add, s_mul, dec = map(per_head, (q_eff, o_intra, s_add, s_mul, dec))
    st = state_ref[...]
    outs = []
    for n in range(nc):
        st16 = st.astype(BF16)
        outs.append(_bmm(q_eff[:, n], st16) + o_intra[:, n])
        st = dec[:, n] * st - _bmm(s_mul[:, n], st16) + s_add[:, n]
    state_ref[...] = st
    o = jnp.stack(outs, axis=1).reshape(nh, ts, GDN_DV)
    gn = gn_ref[...]
    for hd in range(nh):
        sl = slice(hd * GDN_DV, (hd + 1) * GDN_DV)
        z = z_ref[0, :, sl].astype(F32)
        o_ref[0, :, sl] = (_rms(o[hd]) * gn * _silu(z)).astype(BF16)


def _gdn_core(q, k, v, z, bg, out_norm, *, batch, seq, ts):
    width = GDN_HEADS * GDN_DK
    shp = lambda a: a.reshape(batch, seq, a.shape[-1])
    blk = pl.BlockSpec((1, ts, width), lambda b, i: (b, i, 0))
    out = pl.pallas_call(
        _gdn_core_kernel,
        grid=(batch, seq // ts),
        in_specs=[blk, blk, blk, blk,
                  pl.BlockSpec((1, ts, LANES), lambda b, i: (b, i, 0)),
                  pl.BlockSpec((1, GDN_DV), lambda b, i: (0, 0))],
        out_specs=blk,
        out_shape=jax.ShapeDtypeStruct((batch, seq, GDN_HEADS * GDN_DV), BF16),
        scratch_shapes=[pltpu.VMEM((GDN_HEADS, GDN_DK, GDN_DV), F32)],
        compiler_params=_params("parallel", "arbitrary"),
        name="gdn_core",
    )(shp(q), shp(k), shp(v), shp(z), shp(bg), out_norm.reshape(1, GDN_DV))
    return out.reshape(batch * seq, GDN_HEADS * GDN_DV)


def _moe_kernel(x_ref, gain_ref, sc_ref, sh_ref, gate_ref, wr_ref, wg_ref, wu_ref, wd_ref, fn_ref,
                o_ref, h_ref, comb_ref, acc_ref):
    e = pl.program_id(1)
    tm = x_ref.shape[0]
    lane = lax.broadcasted_iota(jnp.int32, (tm, LANES), 1)

    @pl.when(e == 0)
    def _():
        h = _mod_norm(x_ref[...], gain_ref[...], sc_ref[...], sh_ref[...])
        h_ref[...] = h.astype(BF16)
        h_hi, h_mid, _ = _split3(h)
        logits = _dot(h_hi, wr_ref[0]) + (_dot(h_hi, wr_ref[1]) + _dot(h_mid, wr_ref[0]))
        lane_f = lane.astype(F32)
        logits = jnp.where(lane < N_EXPERTS, logits, -jnp.inf)
        m1 = jnp.max(logits, axis=-1, keepdims=True)
        i1 = jnp.min(jnp.where(logits == m1, lane_f, float(LANES)), axis=-1, keepdims=True)
        rest = jnp.where(lane_f == i1, -jnp.inf, logits)
        m2 = jnp.max(rest, axis=-1, keepdims=True)
        i2 = jnp.min(jnp.where(rest == m2, lane_f, float(LANES)), axis=-1, keepdims=True)
        e2 = jnp.exp(m2 - m1)
        w1 = 1.0 / (1.0 + e2)
        comb_ref[...] = jnp.where(lane_f == i1, w1, jnp.where(lane_f == i2, e2 * w1, 0.0))
        acc_ref[...] = jnp.zeros(acc_ref.shape, F32)

    y = _swiglu(h_ref[...], wg_ref.at[0], wu_ref.at[0], wd_ref.at[0], 1)
    c_e = jnp.sum(jnp.where(lane == e, comb_ref[...], 0.0), axis=-1, keepdims=True)
    acc_ref[...] += c_e * y

    @pl.when(e == N_EXPERTS - 1)
    def _():
        xo = x_ref[...] + gate_ref[...] * acc_ref[...]
        o_ref[...] = _rms(xo) * fn_ref[...]


def _moe(x2, mod3, gain, w_router, w_gate_up, w_down, final_norm, *, layer, batch, seq, tm):
    t, d = x2.shape
    n_e, fe, _ = w_down.shape
    tiles_per_seq = seq // tm
    wr = jnp.pad(w_router, ((0, 0), (0, LANES - n_e)))
    wr_hi = wr.astype(BF16)
    wr_lo = (wr - wr_hi.astype(F32)).astype(BF16)
    wr2 = jnp.stack([wr_hi, wr_lo])
    wgu = w_gate_up.astype(BF16)
    return pl.pallas_call(
        _moe_kernel,
        grid=(t // tm, n_e),
        in_specs=[
            pl.BlockSpec((tm, d), lambda i, e: (i, 0)),
            pl.BlockSpec((1, d), lambda i, e: (0, 0)),
            _mod_spec(layer, 4, batch, tiles_per_seq, d),
            _mod_spec(layer, 3, batch, tiles_per_seq, d),
            _mod_spec(layer, 5, batch, tiles_per_seq, d),
            pl.BlockSpec((2, d, LANES), lambda i, e: (0, 0, 0)),
            pl.BlockSpec((1, d, fe), lambda i, e: (e, 0, 0)),
            pl.BlockSpec((1, d, fe), lambda i, e: (e, 0, 1)),
            pl.BlockSpec((1, fe, d), lambda i, e: (e, 0, 0)),
            pl.BlockSpec((1, d), lambda i, e: (0, 0)),
        ],
        out_specs=pl.BlockSpec((tm, d), lambda i, e: (i, 0)),
        out_shape=jax.ShapeDtypeStruct((t, d), F32),
        scratch_shapes=[pltpu.VMEM((tm, d), BF16), pltpu.VMEM((tm, LANES), F32), pltpu.VMEM((tm, d), F32)],
        compiler_params=_params("parallel", "arbitrary"),
        name="moe",
    )(x2, gain.reshape(1, d), mod3, mod3, mod3, wr2, wgu, wgu, w_down.astype(BF16),
      final_norm.reshape(1, d))


def kernel(x, c, positions, ada_w, ada_b, norm_mix, norm_ffn, mla_w_in, mla_q_norm, mla_w_qb, mla_kv_norm, mla_w_kvb, mla_w_out, ffn_w_gate_up, ffn_w_down, gdn_w_in, gdn_conv_w, gdn_a_log, gdn_dt_bias, gdn_out_norm, gdn_w_out, moe_w_router, moe_w_gate_up, moe_w_down, final_norm):
    batch, seq, d = x.shape
    depth = ada_w.shape[0]
    assert depth == 2 and seq % 512 == 0
    t = batch * seq
    tm = 512
    dims = dict(batch=batch, seq=seq, tm=tm)

    mod = _adaln(c, ada_w, ada_b)
    mod3 = mod.reshape(depth * batch * N_MOD, 1, d)
    x2 = x.reshape(t, d)
    pos = positions.astype(F32).reshape(t, 1)

    tq = 256
    q, k, vt = _mla_proj(x2, pos, mod3, norm_mix[0], mla_w_in[0], mla_q_norm[0], mla_w_qb[0],
                         mla_kv_norm[0], mla_w_kvb[0], layer=0, tkv=tq, **dims)
    attn = _mla_attention(q, k, vt, batch=batch, seq=seq, tq=tq, n_heads=4)
    x2 = _out_proj(attn, mla_w_out[0], x2, mod3, layer=0, which=2, name="mla_out", **dims)
    x2 = _ffn(x2, mod3, norm_ffn[0], ffn_w_gate_up[0], ffn_w_down[0], layer=0, **dims)

    gq, gk, gv, gz, bg = _gdn_proj(x2, mod3, norm_mix[1], gdn_w_in[0], gdn_conv_w[0], gdn_a_log[0],
                                   gdn_dt_bias[0], layer=1, **dims)
    og = _gdn_core(gq, gk, gv, gz, bg, gdn_out_norm[0], batch=batch, seq=seq, ts=256)
    x2 = _out_proj(og, gdn_w_out[0], x2, mod3, layer=1, which=2, name="gdn_out", **dims)
    out = _moe(x2, mod3, norm_ffn[1], moe_w_router[0], moe_w_gate_up[0], moe_w_down[0], final_norm,
               layer=1, **dims)
    return out.reshape(batch, seq, d)
```

```python
import functools
import math

import jax
import jax.numpy as jnp
from jax import lax
from jax.experimental import pallas as pl
from jax.experimental.pallas import tpu as pltpu

F32 = jnp.float32
BF16 = jnp.bfloat16

EPS = 1e-6
N_MOD = 6
LANES = 128
VMEM_LIMIT_BYTES = 56 * 1024 * 1024

MLA_HEADS = 16
Q_LORA = 512
KV_LORA = 256
QK_NOPE = 64
QK_ROPE = 32
V_HEAD = 64
ROPE_THETA = 10000.0
MASK_VALUE = -1e30
GDN_HEADS = 8
GDN_DK = 128
GDN_DV = 128
CONV_K = 4
CHUNK = 64
GDN_GROUP = GDN_HEADS
GDN_PROJ_COLS = 256
N_EXPERTS = 8
MOE_TOP_K = 2
MOE_GRANULE = 16
MOE_LHS_ROWS = 512


def _params(*sem):
    return pltpu.CompilerParams(dimension_semantics=sem, vmem_limit_bytes=VMEM_LIMIT_BYTES)


def _resident(shape, index=None):
    index = (0,) * len(shape) if index is None else index
    return pl.BlockSpec(shape, lambda *_: index, pipeline_mode=pl.Buffered(1))


def _split3(a):
    hi = a.astype(BF16)
    r = a - hi.astype(F32)
    mid = r.astype(BF16)
    lo = (r - mid.astype(F32)).astype(BF16)
    return hi, mid, lo


def _dot(a, b):
    return jnp.dot(a, b, preferred_element_type=F32)


def _dot_nt(a, b):
    return lax.dot_general(a, b, (((1,), (1,)), ((), ())), preferred_element_type=F32)


def _dot_tn(a, b):
    return lax.dot_general(a, b, (((0,), (0,)), ((), ())), preferred_element_type=F32)


def _bmm(a, b):
    return lax.dot_general(a, b, (((2,), (1,)), ((0,), (0,))), preferred_element_type=F32)


def _bmm_nt(a, b):
    return lax.dot_general(a, b, (((2,), (2,)), ((0,), (0,))), preferred_element_type=F32)


def _bmm_tn(a, b):
    return lax.dot_general(a, b, (((1,), (1,)), ((0,), (0,))), preferred_element_type=F32)


def _dot_f32(a, b):
    a_hi, a_mid, _ = _split3(a)
    b_hi, b_mid, _ = _split3(b)
    return _dot(a_hi, b_hi) + (_dot(a_hi, b_mid) + _dot(a_mid, b_hi))


def _rms(x):
    return x * lax.rsqrt(jnp.mean(x * x, axis=-1, keepdims=True) + EPS)


def _silu(x):
    return x * jax.nn.sigmoid(x)


def _mod_norm(x, gain, sc, sh):
    return _rms(x) * gain * (1.0 + sc) + sh


def _adaln_kernel(c_ref, w_ref, b_ref, o_ref):
    c = c_ref[...]
    cond = _silu(c)
    o_ref[0] = _dot_f32(cond, w_ref[0]) + b_ref[0]


def _adaln(c, ada_w, ada_b):
    depth, d, n = ada_w.shape
    b = c.shape[0]
    tn = d
    return pl.pallas_call(
        _adaln_kernel,
        grid=(depth, n // tn),
        in_specs=[
            pl.BlockSpec((b, d), lambda l, j: (0, 0)),
            pl.BlockSpec((1, d, tn), lambda l, j: (l, 0, j)),
            pl.BlockSpec((1, 1, tn), lambda l, j: (l, 0, j)),
        ],
        out_specs=pl.BlockSpec((1, b, tn), lambda l, j: (l, 0, j)),
        out_shape=jax.ShapeDtypeStruct((depth, b, n), F32),
        compiler_params=_params("parallel", "parallel"),
        name="adaln",
    )(c, ada_w, ada_b.reshape(depth, 1, n))


def _mod_spec(layer, which, batch, tiles_per_seq, d):
    def index(i, *_):
        return ((layer * batch + i // tiles_per_seq) * N_MOD + which, 0, 0)
    return pl.BlockSpec((None, 1, d), index)


def _mla_proj_kernel(x_ref, pos_ref, gain_ref, sc_ref, sh_ref, win_ref, qn_ref, wqb_ref,
                     kvn_ref, wk_ref, wv_ref, freq_ref, sign_ref, q_out, k_out, vt_out, *, scale):
    tm = x_ref.shape[0]
    h = _mod_norm(x_ref[...], gain_ref[...], sc_ref[...], sh_ref[...]).astype(BF16)
    proj = _dot(h, win_ref[...])
    q_lat = proj[:, :Q_LORA]
    kv_lat = proj[:, Q_LORA:Q_LORA + KV_LORA]
    k_rope = proj[:, Q_LORA + KV_LORA:]
    qn = (_rms(q_lat) * qn_ref[...]).astype(BF16)
    kvn = (_rms(kv_lat) * kvn_ref[...]).astype(BF16)

    ang = pos_ref[...] * freq_ref[...]
    cos = jnp.cos(ang)
    sin = jnp.sin(ang) * sign_ref[...]
    lane = lax.broadcasted_iota(jnp.int32, (tm, LANES), 1)
    first_half = lane < QK_NOPE + QK_ROPE // 2

    def rope(t):
        partner = jnp.where(first_half, pltpu.roll(t, LANES - QK_ROPE // 2, 1),
                            pltpu.roll(t, QK_ROPE // 2, 1))
        return t * cos + partner * sin

    q = _dot(qn, wqb_ref[...])
    k = _dot(kvn, wk_ref[...])
    k_rope = rope(k_rope)
    for hd in range(MLA_HEADS):
        sl = slice(hd * LANES, (hd + 1) * LANES)
        q_out[:, sl] = (rope(q[:, sl]) * scale).astype(BF16)
        k_out[:, sl] = (k[:, sl] + k_rope).astype(BF16)
    vt = _dot(kvn, wv_ref[...]).T
    tkv = vt_out.shape[2]
    for j in range(vt_out.shape[0]):
        vt_out[j] = vt[:, j * tkv:(j + 1) * tkv].astype(BF16)


def _mla_proj(x2, pos, mod3, gain, w_in, q_norm, w_qb, kv_norm, w_kvb, *, layer, batch, seq, tm, tkv):
    t, d = x2.shape
    tiles_per_seq = seq // tm
    hq = MLA_HEADS * LANES
    pad_in = jnp.zeros((d, LANES), F32).at[:, QK_NOPE:QK_NOPE + QK_ROPE].set(w_in[:, Q_LORA + KV_LORA:])
    w_in_p = jnp.concatenate([w_in[:, :Q_LORA + KV_LORA], pad_in], axis=1).astype(BF16)
    w_qb_p = jnp.pad(w_qb.reshape(Q_LORA, MLA_HEADS, QK_NOPE + QK_ROPE),
                     ((0, 0), (0, 0), (0, LANES - QK_NOPE - QK_ROPE))).reshape(Q_LORA, hq).astype(BF16)
    w_kv3 = w_kvb.reshape(KV_LORA, MLA_HEADS, QK_NOPE + V_HEAD)
    w_k_p = jnp.pad(w_kv3[:, :, :QK_NOPE], ((0, 0), (0, 0), (0, LANES - QK_NOPE))
                    ).reshape(KV_LORA, hq).astype(BF16)
    w_v = w_kv3[:, :, QK_NOPE:].reshape(KV_LORA, MLA_HEADS * V_HEAD).astype(BF16)
    half = QK_ROPE // 2
    inv_freq = ROPE_THETA ** (-jnp.arange(half, dtype=F32) / half)
    zeros = jnp.zeros((QK_NOPE,), F32)
    tail = jnp.zeros((LANES - QK_NOPE - QK_ROPE,), F32)
    freq = jnp.concatenate([zeros, inv_freq, inv_freq, tail]).reshape(1, LANES)
    sign = jnp.concatenate([zeros, -jnp.ones((half,), F32), jnp.ones((half,), F32), tail]).reshape(1, LANES)
    scale = float(QK_NOPE + QK_ROPE) ** -0.5 * math.log2(math.e)

    const = _resident
    return pl.pallas_call(
        functools.partial(_mla_proj_kernel, scale=scale),
        grid=(t // tm,),
        in_specs=[
            pl.BlockSpec((tm, d), lambda i: (i, 0)),
            pl.BlockSpec((tm, 1), lambda i: (i, 0)),
            const((1, d)),
            _mod_spec(layer, 1, batch, tiles_per_seq, d),
            _mod_spec(layer, 0, batch, tiles_per_seq, d),
            const(w_in_p.shape), const((1, Q_LORA)), const(w_qb_p.shape),
            const((1, KV_LORA)), const(w_k_p.shape), const(w_v.shape),
            const((1, LANES)), const((1, LANES)),
        ],
        out_specs=[
            pl.BlockSpec((tm, hq), lambda i: (i, 0)),
            pl.BlockSpec((tm, hq), lambda i: (i, 0)),
            pl.BlockSpec((tm // tkv, MLA_HEADS * V_HEAD, tkv), lambda i: (i, 0, 0)),
        ],
        out_shape=[
            jax.ShapeDtypeStruct((t, hq), BF16),
            jax.ShapeDtypeStruct((t, hq), BF16),
            jax.ShapeDtypeStruct((t // tkv, MLA_HEADS * V_HEAD, tkv), BF16),
        ],
        compiler_params=_params("parallel"),
        name="mla_proj",
    )(x2, pos, gain.reshape(1, d), mod3, mod3, w_in_p, q_norm.reshape(1, Q_LORA), w_qb_p,
      kv_norm.reshape(1, KV_LORA), w_k_p, w_v, freq, sign)


def _attn_kernel(q_ref, k_ref, vt_ref, o_ref, s_a, s_b, p_a, p_b, acc_ref, stat_ref, *, tq, n_heads):
    qi = pl.program_id(2)
    kv_idx = lax.broadcasted_iota(jnp.int32, (tq, tq), 0)
    q_idx = lax.broadcasted_iota(jnp.int32, (tq, tq), 1)
    causal = kv_idx <= q_idx
    heads = range(n_heads)

    def scores_into(j, s_ref):
        r0 = pl.multiple_of(j * tq, tq)
        for h in heads:
            q = q_ref[0, :, h * LANES:(h + 1) * LANES]
            s_ref[h] = _dot_nt(k_ref[0, pl.ds(r0, tq), h * LANES:(h + 1) * LANES], q)

    def values(j, p):
        return [_dot(vt_ref[j, h * V_HEAD:(h + 1) * V_HEAD, :], p(h)) for h in heads]

    def softmax(s_ref, masked):
        p_all = []
        for h in heads:
            s = s_ref[h]
            if masked:
                s = jnp.where(causal, s, MASK_VALUE)
            m_old = stat_ref[h, 0:1, :]
            m_new = jnp.maximum(m_old, jnp.max(s, axis=0, keepdims=True))
            p = jnp.exp2(s - m_new)
            alpha = jnp.exp2(m_old - m_new)
            stat_ref[h, 0:1, :] = m_new
            stat_ref[h, 1:2, :] = alpha * stat_ref[h, 1:2, :] + jnp.sum(p, axis=0, keepdims=True)
            stat_ref[h, 2:3, :] = alpha
            p_all.append(p.astype(BF16))
        return p_all

    def step(j, s_cur, p_cur, s_next, p_prev):
        pv = values(jnp.maximum(j - 1, 0), lambda h: p_prev[h])
        alpha_prev = [stat_ref[h, 2:3, :] for h in heads]
        scores_into(j + 1, s_next)
        p = softmax(s_cur, masked=False)
        for h in heads:
            p_cur[h] = p[h]
            acc_ref[h] = alpha_prev[h] * acc_ref[h] + pv[h]

    def finish(s_cur, p_prev):
        pv = values(jnp.maximum(qi - 1, 0), lambda h: p_prev[h])
        alpha_prev = [stat_ref[h, 2:3, :] for h in heads]
        p = softmax(s_cur, masked=True)
        pv_last = values(qi, lambda h: p[h])
        out = []
        for h in heads:
            acc = stat_ref[h, 2:3, :] * (alpha_prev[h] * acc_ref[h] + pv[h]) + pv_last[h]
            out.append(acc / stat_ref[h, 1:2, :])
        for pair in range(n_heads // 2):
            o_t = jnp.concatenate(out[2 * pair:2 * pair + 2], axis=0)
            o_ref[0, :, pair * LANES:(pair + 1) * LANES] = o_t.T.astype(BF16)

    row3 = lax.broadcasted_iota(jnp.int32, stat_ref.shape, 1)
    stat_ref[...] = jnp.where(row3 == 0, MASK_VALUE, jnp.where(row3 == 2, 1.0, 0.0))
    acc_ref[...] = jnp.zeros(acc_ref.shape, F32)
    p_b[...] = jnp.zeros(p_b.shape, BF16)
    scores_into(0, s_a)

    def pair_of_steps(jj, _):
        step(2 * jj, s_a, p_a, s_b, p_b)
        step(2 * jj + 1, s_b, p_b, s_a, p_a)
        return 0

    lax.fori_loop(0, qi // 2, pair_of_steps, 0)

    @pl.when(qi % 2 == 1)
    def _():
        step(qi - 1, s_a, p_a, s_b, p_b)
        finish(s_b, p_a)

    @pl.when(qi % 2 == 0)
    def _():
        finish(s_a, p_b)


def _mla_attention(q, k, vt, *, batch, seq, tq, n_heads):
    hq = MLA_HEADS * LANES
    q3 = q.reshape(batch, seq, hq)
    k3 = k.reshape(batch, seq, hq)
    n_kv = seq // tq
    out = pl.pallas_call(
        functools.partial(_attn_kernel, tq=tq, n_heads=n_heads),
        grid=(batch, MLA_HEADS // n_heads, seq // tq),
        in_specs=[
            pl.BlockSpec((1, tq, n_heads * LANES), lambda b, h, i: (b, i, h)),
            pl.BlockSpec((1, seq, n_heads * LANES), lambda b, h, i: (b, 0, h)),
            pl.BlockSpec((n_kv, n_heads * V_HEAD, tq), lambda b, h, i: (b, h, 0)),
        ],
        out_specs=pl.BlockSpec((1, tq, n_heads * V_HEAD), lambda b, h, i: (b, i, h)),
        out_shape=jax.ShapeDtypeStruct((batch, seq, MLA_HEADS * V_HEAD), BF16),
        scratch_shapes=[
            pltpu.VMEM((n_heads, tq, tq), F32), pltpu.VMEM((n_heads, tq, tq), F32),
            pltpu.VMEM((n_heads, tq, tq), BF16), pltpu.VMEM((n_heads, tq, tq), BF16),
            pltpu.VMEM((n_heads, V_HEAD, tq), F32), pltpu.VMEM((n_heads, 8, tq), F32),
        ],
        compiler_params=_params("parallel", "parallel", "parallel"),
        name="mla_attn",
    )(q3, k3, vt)
    return out.reshape(batch * seq, MLA_HEADS * V_HEAD)


def _out_proj_kernel(a_ref, w_ref, x_ref, gate_ref, o_ref):
    o_ref[...] = x_ref[...] + gate_ref[...] * _dot(a_ref[...], w_ref[...])


def _out_proj(a, w, x2, mod3, *, layer, which, batch, seq, tm, name):
    t, d = x2.shape
    kdim = a.shape[1]
    return pl.pallas_call(
        _out_proj_kernel,
        grid=(t // tm,),
        in_specs=[
            pl.BlockSpec((tm, kdim), lambda i: (i, 0)),
            _resident((kdim, d)),
            pl.BlockSpec((tm, d), lambda i: (i, 0)),
            _mod_spec(layer, which, batch, seq // tm, d),
        ],
        out_specs=pl.BlockSpec((tm, d), lambda i: (i, 0)),
        out_shape=jax.ShapeDtypeStruct((t, d), F32),
        compiler_params=_params("parallel"),
        name=name,
    )(a, w.astype(BF16), x2, mod3)


def _swiglu(h, wg_ref, wu_ref, wd_ref, n_chunks):
    f = wd_ref.shape[0]
    tf = f // n_chunks
    acc = None
    for j in range(n_chunks):
        sl = slice(j * tf, (j + 1) * tf)
        g = _dot(h, wg_ref[:, sl])
        u = _dot(h, wu_ref[:, sl])
        part = _dot((_silu(g) * u).astype(BF16), wd_ref[sl, :])
        acc = part if acc is None else acc + part
    return acc


def _ffn_kernel(x_ref, gain_ref, sc_ref, sh_ref, gate_ref, wg_ref, wu_ref, wd_ref, o_ref, *, n_chunks):
    x = x_ref[...]
    h = _mod_norm(x, gain_ref[...], sc_ref[...], sh_ref[...]).astype(BF16)
    o_ref[...] = x + gate_ref[...] * _swiglu(h, wg_ref, wu_ref, wd_ref, n_chunks)


def _ffn(x2, mod3, gain, w_gate_up, w_down, *, layer, batch, seq, tm):
    t, d = x2.shape
    f = w_down.shape[0]
    tiles_per_seq = seq // tm
    wgu = w_gate_up.astype(BF16)
    return pl.pallas_call(
        functools.partial(_ffn_kernel, n_chunks=2),
        grid=(t // tm,),
        in_specs=[
            pl.BlockSpec((tm, d), lambda i: (i, 0)),
            pl.BlockSpec((1, d), lambda i: (0, 0)),
            _mod_spec(layer, 4, batch, tiles_per_seq, d),
            _mod_spec(layer, 3, batch, tiles_per_seq, d),
            _mod_spec(layer, 5, batch, tiles_per_seq, d),
            _resident((d, f)),
            _resident((d, f), (0, 1)),
            _resident((f, d)),
        ],
        out_specs=pl.BlockSpec((tm, d), lambda i: (i, 0)),
        out_shape=jax.ShapeDtypeStruct((t, d), F32),
        compiler_params=_params("parallel"),
        name="ffn",
    )(x2, gain.reshape(1, d), mod3, mod3, mod3, wgu, wgu, w_down.astype(BF16))


def _gdn_proj_kernel(x_ref, gain_ref, sc_ref, sh_ref, w_ref, conv_ref, alog_ref, dtb_ref,
                     q_out, k_out, v_out, z_out, bg_out, tail_ref, *, tiles_per_seq):
    tm = x_ref.shape[0]
    nqk = GDN_HEADS * GDN_DK
    nqkv = 2 * nqk + GDN_HEADS * GDN_DV
    nz = GDN_HEADS * GDN_DV
    i = pl.program_id(0)
    h = _mod_norm(x_ref[...], gain_ref[...], sc_ref[...], sh_ref[...]).astype(BF16)

    @pl.when(i % tiles_per_seq == 0)
    def _():
        tail_ref[...] = jnp.zeros(tail_ref.shape, F32)

    def l2n(t):
        return t * lax.rsqrt(jnp.sum(t * t, axis=-1, keepdims=True) + EPS)

    for c0 in range(0, nqkv, GDN_PROJ_COLS):
        cols = slice(c0, c0 + GDN_PROJ_COLS)
        cur = _dot(h, w_ref[:, cols])
        ext = jnp.concatenate([tail_ref[:, cols], cur], axis=0)
        tail_ref[:, cols] = cur[tm - 8:, :]
        y = conv_ref[CONV_K - 1:CONV_K, cols] * cur
        for back in range(1, CONV_K):
            tap = CONV_K - 1 - back
            y = y + conv_ref[tap:tap + 1, cols] * pltpu.roll(ext, back, 0)[8:, :]
        act = _silu(y)
        for h0 in range(0, GDN_PROJ_COLS, GDN_DK):
            head = act[:, h0:h0 + GDN_DK]
            dst = slice(c0 % nqk + h0, c0 % nqk + h0 + GDN_DK)
            if c0 < nqk:
                q_out[:, dst] = (l2n(head) * (GDN_DK ** -0.5)).astype(BF16)
            elif c0 < 2 * nqk:
                k_out[:, dst] = l2n(head).astype(BF16)
            else:
                v_out[:, dst] = head.astype(BF16)
    for c0 in range(0, nz, GDN_PROJ_COLS):
        z_out[:, c0:c0 + GDN_PROJ_COLS] = _dot(h, w_ref[:, nqkv + c0:nqkv + c0 + GDN_PROJ_COLS]).astype(BF16)

    ba = _dot(h, w_ref[:, nqkv + nz:])
    lane = lax.broadcasted_iota(jnp.int32, ba.shape, 1) & (LANES - 1)
    beta = jax.nn.sigmoid(ba)
    pre = ba + dtb_ref[...]
    softplus = jnp.maximum(pre, 0.0) + jnp.log(1.0 + jnp.exp(-jnp.abs(pre)))
    g = jnp.where((lane >= GDN_GROUP) & (lane < 2 * GDN_GROUP), -jnp.exp(alog_ref[...]) * softplus, 0.0)
    row = lax.broadcasted_iota(jnp.int32, (tm, tm), 0)
    col = lax.broadcasted_iota(jnp.int32, (tm, tm), 1)
    same_chunk = (row & -CHUNK) == (col & -CHUNK)
    tril = jnp.where(same_chunk & (col <= row), 1.0, 0.0).astype(BF16)
    g_hi, g_mid, g_lo = _split3(g)
    gc = _dot(tril, g_hi) + (_dot(tril, g_mid) + _dot(tril, g_lo))
    bg_out[...] = jnp.where(lane < GDN_GROUP, beta, gc)


def _gdn_proj(x2, mod3, gain, w_in, conv_w, a_log, dt_bias, *, layer, batch, seq, tm):
    t, d = x2.shape
    nqk = GDN_HEADS * GDN_DK
    nqkv = 2 * nqk + GDN_HEADS * GDN_DV
    nz = GDN_HEADS * GDN_DV
    n_groups = GDN_HEADS // GDN_GROUP
    tiles_per_seq = seq // tm
    wb = w_in[:, nqkv + nz:nqkv + nz + GDN_HEADS]
    wa = w_in[:, nqkv + nz + GDN_HEADS:]

    def group_rows(b_part, a_part, dtype):
        rows = b_part.shape[0]
        chunks = []
        for gidx in range(n_groups):
            sl = slice(gidx * GDN_GROUP, (gidx + 1) * GDN_GROUP)
            chunks += [b_part[:, sl], a_part[:, sl], jnp.zeros((rows, LANES - 2 * GDN_GROUP), dtype)]
        return jnp.concatenate(chunks, axis=1)

    w_p = jnp.concatenate([w_in[:, :nqkv + nz], group_rows(wb, wa, F32)], axis=1).astype(BF16)
    zero_h = jnp.zeros((1, GDN_HEADS), F32)
    alog_p = group_rows(zero_h, a_log.reshape(1, GDN_HEADS), F32)
    dtb_p = group_rows(zero_h, dt_bias.reshape(1, GDN_HEADS), F32)
    const = _resident
    row_spec = lambda n: pl.BlockSpec((tm, n), lambda i: (i, 0))
    return pl.pallas_call(
        functools.partial(_gdn_proj_kernel, tiles_per_seq=tiles_per_seq),
        grid=(t // tm,),
        in_specs=[
            row_spec(d), const((1, d)),
            _mod_spec(layer, 1, batch, tiles_per_seq, d),
            _mod_spec(layer, 0, batch, tiles_per_seq, d),
            const(w_p.shape), const((CONV_K, nqkv)),
            const((1, n_groups * LANES)), const((1, n_groups * LANES)),
        ],
        out_specs=[row_spec(nqk), row_spec(nqk), row_spec(nz), row_spec(nz), row_spec(n_groups * LANES)],
        out_shape=[
            jax.ShapeDtypeStruct((t, nqk), BF16),
            jax.ShapeDtypeStruct((t, nqk), BF16),
            jax.ShapeDtypeStruct((t, nz), BF16),
            jax.ShapeDtypeStruct((t, nz), BF16),
            jax.ShapeDtypeStruct((t, n_groups * LANES), F32),
        ],
        scratch_shapes=[pltpu.VMEM((8, nqkv), F32)],
        compiler_params=_params("arbitrary"),
        name="gdn_proj",
    )(x2, gain.reshape(1, d), mod3, mod3, w_p, conv_w, alog_p, dtb_p)


def _gdn_core_kernel(q_ref, k_ref, v_ref, z_ref, bg_ref, gn_ref, o_ref, state_ref):
    ts = q_ref.shape[1]
    c = CHUNK
    nc = ts // c
    nh = GDN_HEADS
    row = lax.broadcasted_iota(jnp.int32, (1, c, c), 1)
    col = lax.broadcasted_iota(jnp.int32, (1, c, c), 2)
    causal = col <= row
    strict = col < row
    eye = jnp.where(row == col, 1.0, 0.0).astype(F32)

    def level_mask(s):
        return ((row & -2 * s) == (col & -2 * s)) & ((row & s) != 0) & ((col & s) == 0)

    lane = lax.broadcasted_iota(jnp.int32, (nh * nc, c, LANES), 2)

    @pl.when(pl.program_id(1) == 0)
    def _():
        state_ref[...] = jnp.zeros(state_ref.shape, F32)

    bg = bg_ref[0]

    def stack(ref):
        return jnp.concatenate(
            [ref[0, :, hd * GDN_DK:(hd + 1) * GDN_DK].reshape(nc, c, GDN_DK) for hd in range(nh)], axis=0)

    def stack_col(first_lane):
        return jnp.concatenate(
            [bg[:, first_lane + hd:first_lane + hd + 1].reshape(nc, c, 1) for hd in range(nh)], axis=0)

    q = stack(q_ref).astype(F32)
    k16 = stack(k_ref)
    k = k16.astype(F32)
    v = stack(v_ref).astype(F32)
    beta = stack_col(0)
    gc = stack_col(nh)

    g_hi, g_mid, g_lo = (t.astype(F32) for t in _split3(gc))
    terms = lambda base, other: jnp.where(lane == base, g_hi, jnp.where(
        lane == base + 1, g_mid, jnp.where(lane == base + 2, g_lo, other)))
    lhs = terms(0, jnp.where(lane < 6, 1.0, 0.0)).astype(BF16)
    rhs = terms(3, jnp.where(lane < 3, -1.0, 0.0)).astype(BF16)
    diff = -_bmm_nt(lhs, rhs)
    decay = jnp.where(causal, jnp.exp(jnp.where(causal, diff, 0.0)), 0.0)

    kb = k * beta
    a_mat = jnp.where(strict, _bmm_nt(kb.astype(BF16), k16) * decay, 0.0)
    attn = jnp.where(causal, _bmm_nt(q.astype(BF16), k16) * decay, 0.0).astype(BF16)

    inv = eye - jnp.where(level_mask(1), a_mat, 0.0)
    s = 2
    while s < c:
        x_s = jnp.where(level_mask(s), a_mat, 0.0).astype(BF16)
        inv16 = inv.astype(BF16)
        inv = inv - _bmm(_bmm(inv16, x_s).astype(BF16), inv16)
        s *= 2

    eg = jnp.exp(gc)
    g_last = gc[:, c - 1:c, :]
    rhs_uw = jnp.concatenate([v * beta, kb * eg], axis=-1).astype(BF16)
    uw = _bmm(inv.astype(BF16), rhs_uw)
    a_uw = _bmm(attn, uw.astype(BF16))
    o_intra = a_uw[..., :GDN_DV]
    q_eff = (q * eg - a_uw[..., GDN_DV:]).astype(BF16)
    kt_uw = _bmm_tn(k16, (uw * jnp.exp(g_last - gc)).astype(BF16))
    s_add = kt_uw[..., :GDN_DV]
    s_mul = kt_uw[..., GDN_DV:].astype(BF16)
    dec = jnp.exp(g_last)

    per_head = lambda a: a.reshape((nh, nc) + a.shape[1:])
    q_eff, o_intra, s_add, s_mul, dec = map(per_head, (q_eff, o_intra, s_add, s_mul, dec))
    st = state_ref[...]
    outs = []
    for n in range(nc):
        st16 = st.astype(BF16)
        outs.append(_bmm(q_eff[:, n], st16) + o_intra[:, n])
        st = dec[:, n] * st - _bmm(s_mul[:, n], st16) + s_add[:, n]
    state_ref[...] = st
    o = jnp.stack(outs, axis=1).reshape(nh, ts, GDN_DV)
    gn = gn_ref[...]
    for hd in range(nh):
        sl = slice(hd * GDN_DV, (hd + 1) * GDN_DV)
        z = z_ref[0, :, sl].astype(F32)
        o_ref[0, :, sl] = (_rms(o[hd]) * gn * _silu(z)).astype(BF16)


def _gdn_core(q, k, v, z, bg, out_norm, *, batch, seq, ts):
    width = GDN_HEADS * GDN_DK
    shp = lambda a: a.reshape(batch, seq, a.shape[-1])
    blk = pl.BlockSpec((1, ts, width), lambda b, i: (b, i, 0))
    out = pl.pallas_call(
        _gdn_core_kernel,
        grid=(batch, seq // ts),
        in_specs=[blk, blk, blk, blk,
                  pl.BlockSpec((1, ts, LANES), lambda b, i: (b, i, 0)),
                  pl.BlockSpec((1, GDN_DV), lambda b, i: (0, 0))],
        out_specs=blk,
        out_shape=jax.ShapeDtypeStruct((batch, seq, GDN_HEADS * GDN_DV), BF16),
        scratch_shapes=[pltpu.VMEM((GDN_HEADS, GDN_DK, GDN_DV), F32)],
        compiler_params=_params("parallel", "arbitrary"),
        name="gdn_core",
    )(shp(q), shp(k), shp(v), shp(z), shp(bg), out_norm.reshape(1, GDN_DV))
    return out.reshape(batch * seq, GDN_HEADS * GDN_DV)


def _moe_slots(tb):
    slots = MOE_TOP_K * tb + N_EXPERTS * MOE_GRANULE
    return slots, slots // MOE_GRANULE


def _moe_route_kernel(x_ref, gain_ref, sc_ref, sh_ref, wr_ref, hs_ref, rt_ref, meta_ref):
    tb = x_ref.shape[0]
    slots = hs_ref.shape[0]
    h = _mod_norm(x_ref[...], gain_ref[...], sc_ref[...], sh_ref[...])
    h_hi, h_mid, _ = _split3(h)
    logits = _dot(h_hi, wr_ref[0]) + (_dot(h_hi, wr_ref[1]) + _dot(h_mid, wr_ref[0]))
    lane = lax.broadcasted_iota(jnp.int32, (tb, LANES), 1)
    lane_f = lane.astype(F32)
    logits = jnp.where(lane < N_EXPERTS, logits, -jnp.inf)
    m1 = jnp.max(logits, axis=-1, keepdims=True)
    i1 = jnp.min(jnp.where(logits == m1, lane_f, float(LANES)), axis=-1, keepdims=True)
    rest = jnp.where(lane_f == i1, -jnp.inf, logits)
    m2 = jnp.max(rest, axis=-1, keepdims=True)
    i2 = jnp.min(jnp.where(rest == m2, lane_f, float(LANES)), axis=-1, keepdims=True)
    e2 = jnp.exp(m2 - m1)
    w1 = 1.0 / (1.0 + e2)
    w2 = e2 * w1

    first = lane_f == i1
    second = lane_f == i2
    chosen = jnp.where(first, 1.0, jnp.where(second, 1.0, 0.0))
    r = lax.broadcasted_iota(jnp.int32, (tb, tb), 0)
    c = lax.broadcasted_iota(jnp.int32, (tb, tb), 1)
    earlier = jnp.where(c < r, 1.0, 0.0).astype(BF16)
    rank = _dot(earlier, chosen.astype(BF16))
    count = jnp.sum(chosen, axis=0, keepdims=True)
    granules = jnp.floor((count + (MOE_GRANULE - 1)) * (1.0 / MOE_GRANULE))
    er = lax.broadcasted_iota(jnp.int32, (LANES, LANES), 0)
    ec = lax.broadcasted_iota(jnp.int32, (LANES, LANES), 1)
    before = jnp.where(er < ec, 1.0, 0.0).astype(BF16)
    seg_gran = _dot(jnp.broadcast_to(granules, (8, LANES)).astype(BF16), before)[0:1]
    slot_of = seg_gran * MOE_GRANULE + rank
    pos1 = jnp.sum(jnp.where(first, slot_of, 0.0), axis=-1, keepdims=True)
    pos2 = jnp.sum(jnp.where(second, slot_of, 0.0), axis=-1, keepdims=True)

    slot = lax.broadcasted_iota(jnp.int32, (tb, slots), 1).astype(F32)
    place = jnp.where(slot == pos1, 1.0, jnp.where(slot == pos2, 1.0, 0.0)).astype(BF16)
    hs_ref[...] = _dot_tn(place, h_hi).astype(BF16)
    rt_ref[...] = jnp.where(lane == 0, pos1, jnp.where(lane == 1, pos2, jnp.where(
        lane == 2, w1, jnp.where(lane == 3, w2, 0.0))))
    row8 = lax.broadcasted_iota(jnp.int32, (8, LANES), 0)
    meta_ref[0] = jnp.where(row8 == 0, granules, jnp.where(row8 == 1, seg_gran, 0.0))


def _moe_expert_kernel(src_ref, te_ref, used_ref, *refs):
    n_gran = len(refs) - 4
    wg_ref, wu_ref, wd_ref, o_ref = refs[n_gran:]
    step = pl.program_id(0)

    @pl.when(step < used_ref[0])
    def _():
        h = jnp.concatenate([g[...] for g in refs[:n_gran]], axis=0)
        o_ref[...] = _swiglu(h, wg_ref.at[0], wu_ref.at[0], wd_ref.at[0], 1).astype(BF16)

    @pl.when(step >= used_ref[0])
    def _():
        o_ref[...] = jnp.zeros(o_ref.shape, BF16)


def _moe_combine_kernel(inv_ref, *refs):
    n_gran = len(refs) - 5
    rt_ref, x_ref, gate_ref, fn_ref, o_ref = refs[n_gran:]
    tb = x_ref.shape[0]
    ys = jnp.concatenate([g[...] for g in refs[:n_gran]], axis=0)
    rt = rt_ref[...]
    pos1, pos2, w1, w2 = rt[:, 0:1], rt[:, 1:2], rt[:, 2:3], rt[:, 3:4]
    slot = lax.broadcasted_iota(jnp.int32, (tb, ys.shape[0]), 1).astype(F32)
    y1 = _dot(jnp.where(slot == pos1, 1.0, 0.0).astype(BF16), ys)
    y2 = _dot(jnp.where(slot == pos2, 1.0, 0.0).astype(BF16), ys)
    xo = x_ref[...] + gate_ref[...] * (w1 * y1 + w2 * y2)
    o_ref[...] = _rms(xo) * fn_ref[...]


def _moe_tables(meta, n_steps, gran_per_step, gran_per_tile):
    i32 = jnp.int32
    cnt = meta[:, 0, :N_EXPERTS].astype(i32)
    seg = meta[:, 1, :N_EXPERTS].astype(i32)
    n_tiles = cnt.shape[0]
    csum = jnp.concatenate([jnp.zeros((1, N_EXPERTS), i32), jnp.cumsum(cnt, axis=0)], axis=0)
    total = csum[-1]
    steps = (total + gran_per_step - 1) // gran_per_step
    step_end = jnp.cumsum(steps)
    step_start = step_end - steps
    used = step_end[-1:]

    s = jnp.arange(n_steps * gran_per_step, dtype=i32)
    st = s // gran_per_step
    ex = jnp.minimum(jnp.sum((step_end[None, :] <= st[:, None]).astype(i32), axis=1), N_EXPERTS - 1)
    q = (st - step_start[ex]) * gran_per_step + s % gran_per_step
    valid = (st < used[0]) & (q < total[ex])
    csum_e = csum.T[ex]
    tile = jnp.minimum(jnp.sum((csum_e[:, 1:] <= q[:, None]).astype(i32), axis=1), n_tiles - 1)
    src = tile * gran_per_tile + seg[tile, ex] + (q - jnp.take_along_axis(csum_e, tile[:, None], 1)[:, 0])
    src = jnp.where(valid, src, 0)
    step_expert = ex[::gran_per_step]

    g = jnp.arange(gran_per_tile, dtype=i32)[None, :, None]
    inside = (g >= seg[:, None, :]) & (g < (seg + cnt)[:, None, :])
    owner = jnp.argmax(inside, axis=-1).astype(i32)
    pick = lambda a: jnp.take_along_axis(a, owner, axis=1)
    inv = step_start[owner] * gran_per_step + pick(csum[:-1]) + (g[..., 0] - pick(seg))
    inv = jnp.where(jnp.any(inside, axis=-1), inv, 0).reshape(-1)
    return src, step_expert, used, inv


def _moe(x2, mod3, gain, w_router, w_gate_up, w_down, final_norm, *, layer, batch, seq, tb):
    t, d = x2.shape
    n_e, fe, _ = w_down.shape
    tiles_per_seq = seq // tb
    n_tiles = t // tb
    slots, gran_per_tile = _moe_slots(tb)
    gran_per_step = MOE_LHS_ROWS // MOE_GRANULE
    n_steps = pl.cdiv(MOE_TOP_K * t // MOE_GRANULE + n_tiles * n_e, gran_per_step) + n_e
    wr = jnp.pad(w_router, ((0, 0), (0, LANES - n_e)))
    wr_hi = wr.astype(BF16)
    wr_lo = (wr - wr_hi.astype(F32)).astype(BF16)
    wr2 = jnp.stack([wr_hi, wr_lo])
    wgu = w_gate_up.astype(BF16)
    row_spec = lambda rows, cols: pl.BlockSpec((rows, cols), lambda i, *_: (i, 0))

    hs, rt, meta = pl.pallas_call(
        _moe_route_kernel,
        grid=(n_tiles,),
        in_specs=[
            row_spec(tb, d), _resident((1, d)),
            _mod_spec(layer, 4, batch, tiles_per_seq, d),
            _mod_spec(layer, 3, batch, tiles_per_seq, d),
            _resident((2, d, LANES)),
        ],
        out_specs=[row_spec(slots, d), row_spec(tb, LANES), pl.BlockSpec((1, 8, LANES), lambda i: (i, 0, 0))],
        out_shape=[
            jax.ShapeDtypeStruct((n_tiles * slots, d), BF16),
            jax.ShapeDtypeStruct((t, LANES), F32),
            jax.ShapeDtypeStruct((n_tiles, 8, LANES), F32),
        ],
        compiler_params=_params("parallel"),
        name="moe_route",
    )(x2, gain.reshape(1, d), mod3, mod3, wr2)

    src, step_expert, used, inv = _moe_tables(meta, n_steps, gran_per_step, gran_per_tile)

    def granule_spec(j, per_step):
        return pl.BlockSpec((MOE_GRANULE, d), lambda i, tbl, *_: (tbl[i * per_step + j], 0))

    ye = pl.pallas_call(
        _moe_expert_kernel,
        grid_spec=pltpu.PrefetchScalarGridSpec(
            num_scalar_prefetch=3,
            grid=(n_steps,),
            in_specs=[granule_spec(j, gran_per_step) for j in range(gran_per_step)] + [
                pl.BlockSpec((1, d, fe), lambda i, src, te, used: (te[i], 0, 0)),
                pl.BlockSpec((1, d, fe), lambda i, src, te, used: (te[i], 0, 1)),
                pl.BlockSpec((1, fe, d), lambda i, src, te, used: (te[i], 0, 0)),
            ],
            out_specs=pl.BlockSpec((MOE_LHS_ROWS, d), lambda i, *_: (i, 0)),
        ),
        out_shape=jax.ShapeDtypeStruct((n_steps * MOE_LHS_ROWS, d), BF16),
        compiler_params=_params("arbitrary"),
        name="moe_experts",
    )(src, step_expert, used, *([hs] * gran_per_step), wgu, wgu, w_down.astype(BF16))

    return pl.pallas_call(
        _moe_combine_kernel,
        grid_spec=pltpu.PrefetchScalarGridSpec(
            num_scalar_prefetch=1,
            grid=(n_tiles,),
            in_specs=[granule_spec(j, gran_per_tile) for j in range(gran_per_tile)] + [
                row_spec(tb, LANES), row_spec(tb, d),
                _mod_spec(layer, 5, batch, tiles_per_seq, d),
                pl.BlockSpec((1, d), lambda i, *_: (0, 0)),
            ],
            out_specs=row_spec(tb, d),
        ),
        out_shape=jax.ShapeDtypeStruct((t, d), F32),
        compiler_params=_params("parallel"),
        name="moe_combine",
    )(inv, *([ye] * gran_per_tile), rt, x2, mod3, final_norm.reshape(1, d))


def kernel(x, c, positions, ada_w, ada_b, norm_mix, norm_ffn, mla_w_in, mla_q_norm, mla_w_qb, mla_kv_norm, mla_w_kvb, mla_w_out, ffn_w_gate_up, ffn_w_down, gdn_w_in, gdn_conv_w, gdn_a_log, gdn_dt_bias, gdn_out_norm, gdn_w_out, moe_w_router, moe_w_gate_up, moe_w_down, final_norm):
    batch, seq, d = x.shape
    depth = ada_w.shape[0]
    assert depth == 2 and seq % 512 == 0
    t = batch * seq
    tm = 512
    dims = dict(batch=batch, seq=seq, tm=tm)

    mod = _adaln(c, ada_w, ada_b)
    mod3 = mod.reshape(depth * batch * N_MOD, 1, d)
    x2 = x.reshape(t, d)
    pos = positions.astype(F32).reshape(t, 1)

    tq = 256
    q, k, vt = _mla_proj(x2, pos, mod3, norm_mix[0], mla_w_in[0], mla_q_norm[0], mla_w_qb[0],
                         mla_kv_norm[0], mla_w_kvb[0], layer=0, tkv=tq, **dims)
    attn = _mla_attention(q, k, vt, batch=batch, seq=seq, tq=tq, n_heads=4)
    x2 = _out_proj(attn, mla_w_out[0], x2, mod3, layer=0, which=2, name="mla_out", **dims)
    x2 = _ffn(x2, mod3, norm_ffn[0], ffn_w_gate_up[0], ffn_w_down[0], layer=0, **dims)

    gq, gk, gv, gz, bg = _gdn_proj(x2, mod3, norm_mix[1], gdn_w_in[0], gdn_conv_w[0], gdn_a_log[0],
                                   gdn_dt_bias[0], layer=1, **dims)
    og = _gdn_core(gq, gk, gv, gz, bg, gdn_out_norm[0], batch=batch, seq=seq, ts=256)
    x2 = _out_proj(og, gdn_w_out[0], x2, mod3, layer=1, which=2, name="gdn_out", **dims)
    out = _moe(x2, mod3, norm_ffn[1], moe_w_router[0], moe_w_gate_up[0], moe_w_down[0], final_norm,
               layer=1, batch=batch, seq=seq, tb=256)
    return out.reshape(batch, seq, d)
```

```python
import functools
import math

import jax
import jax.numpy as jnp
from jax import lax
from jax.experimental import pallas as pl
from jax.experimental.pallas import tpu as pltpu

F32 = jnp.float32
BF16 = jnp.bfloat16

EPS = 1e-6
N_MOD = 6
LANES = 128
VMEM_LIMIT_BYTES = 56 * 1024 * 1024

MLA_HEADS = 16
Q_LORA = 512
KV_LORA = 256
QK_NOPE = 64
QK_ROPE = 32
V_HEAD = 64
ROPE_THETA = 10000.0
MASK_VALUE = -1e30
GDN_HEADS = 8
GDN_DK = 128
GDN_DV = 128
CONV_K = 4
CHUNK = 64
GDN_GROUP = GDN_HEADS
GDN_PROJ_COLS = 256
N_EXPERTS = 8
MOE_TOP_K = 2
MOE_GRANULE = 16
MOE_LHS_ROWS = 512


def _params(*sem):
    return pltpu.CompilerParams(dimension_semantics=sem, vmem_limit_bytes=VMEM_LIMIT_BYTES)


def _resident(shape, index=None):
    index = (0,) * len(shape) if index is None else index
    return pl.BlockSpec(shape, lambda *_: index, pipeline_mode=pl.Buffered(1))


def _split3(a):
    hi = a.astype(BF16)
    r = a - hi.astype(F32)
    mid = r.astype(BF16)
    lo = (r - mid.astype(F32)).astype(BF16)
    return hi, mid, lo


def _dot(a, b):
    return jnp.dot(a, b, preferred_element_type=F32)


def _dot_nt(a, b):
    return lax.dot_general(a, b, (((1,), (1,)), ((), ())), preferred_element_type=F32)


def _dot_tn(a, b):
    return lax.dot_general(a, b, (((0,), (0,)), ((), ())), preferred_element_type=F32)


def _bmm(a, b):
    return lax.dot_general(a, b, (((2,), (1,)), ((0,), (0,))), preferred_element_type=F32)


def _bmm_nt(a, b):
    return lax.dot_general(a, b, (((2,), (2,)), ((0,), (0,))), preferred_element_type=F32)


def _bmm_tn(a, b):
    return lax.dot_general(a, b, (((1,), (1,)), ((0,), (0,))), preferred_element_type=F32)


def _dot_f32(a, b):
    a_hi, a_mid, _ = _split3(a)
    b_hi, b_mid, _ = _split3(b)
    return _dot(a_hi, b_hi) + (_dot(a_hi, b_mid) + _dot(a_mid, b_hi))


def _rms(x):
    return x * lax.rsqrt(jnp.mean(x * x, axis=-1, keepdims=True) + EPS)


def _silu(x):
    return x * jax.nn.sigmoid(x)


def _mod_norm(x, gain, sc, sh):
    return _rms(x) * gain * (1.0 + sc) + sh


def _adaln_kernel(c_ref, w_ref, b_ref, o_ref):
    c = c_ref[...]
    cond = _silu(c)
    o_ref[0] = _dot_f32(cond, w_ref[0]) + b_ref[0]


def _adaln(c, ada_w, ada_b):
    depth, d, n = ada_w.shape
    b = c.shape[0]
    tn = d
    return pl.pallas_call(
        _adaln_kernel,
        grid=(depth, n // tn),
        in_specs=[
            pl.BlockSpec((b, d), lambda l, j: (0, 0)),
            pl.BlockSpec((1, d, tn), lambda l, j: (l, 0, j)),
            pl.BlockSpec((1, 1, tn), lambda l, j: (l, 0, j)),
        ],
        out_specs=pl.BlockSpec((1, b, tn), lambda l, j: (l, 0, j)),
        out_shape=jax.ShapeDtypeStruct((depth, b, n), F32),
        compiler_params=_params("parallel", "parallel"),
        name="adaln",
    )(c, ada_w, ada_b.reshape(depth, 1, n))


def _mod_spec(layer, which, batch, tiles_per_seq, d):
    def index(i, *_):
        return ((layer * batch + i // tiles_per_seq) * N_MOD + which, 0, 0)
    return pl.BlockSpec((None, 1, d), index)


def _mla_proj_kernel(x_ref, pos_ref, gain_ref, sc_ref, sh_ref, win_ref, qn_ref, wqb_ref,
                     kvn_ref, wk_ref, wv_ref, freq_ref, sign_ref, q_out, k_out, vt_out, *, scale):
    tm = x_ref.shape[0]
    h = _mod_norm(x_ref[...], gain_ref[...], sc_ref[...], sh_ref[...]).astype(BF16)
    proj = _dot(h, win_ref[...])
    q_lat = proj[:, :Q_LORA]
    kv_lat = proj[:, Q_LORA:Q_LORA + KV_LORA]
    k_rope = proj[:, Q_LORA + KV_LORA:]
    qn = (_rms(q_lat) * qn_ref[...]).astype(BF16)
    kvn = (_rms(kv_lat) * kvn_ref[...]).astype(BF16)

    ang = pos_ref[...] * freq_ref[...]
    cos = jnp.cos(ang)
    sin = jnp.sin(ang) * sign_ref[...]
    lane = lax.broadcasted_iota(jnp.int32, (tm, LANES), 1)
    first_half = lane < QK_NOPE + QK_ROPE // 2

    def rope(t, cos, sin):
        partner = jnp.where(first_half, pltpu.roll(t, LANES - QK_ROPE // 2, 1),
                            pltpu.roll(t, QK_ROPE // 2, 1))
        return t * cos + partner * sin

    q = _dot(qn, wqb_ref[...])
    k = _dot(kvn, wk_ref[...])
    k_rope = rope(k_rope, cos, sin)
    cos_q = cos * scale
    sin_q = sin * scale
    for hd in range(MLA_HEADS):
        sl = slice(hd * LANES, (hd + 1) * LANES)
        q_out[:, sl] = rope(q[:, sl], cos_q, sin_q).astype(BF16)
        k_out[:, sl] = (k[:, sl] + k_rope).astype(BF16)
    vt = _dot(kvn, wv_ref[...]).T
    tkv = vt_out.shape[2]
    for j in range(vt_out.shape[0]):
        vt_out[j] = vt[:, j * tkv:(j + 1) * tkv].astype(BF16)


def _mla_proj(x2, pos, mod3, gain, w_in, q_norm, w_qb, kv_norm, w_kvb, *, layer, batch, seq, tm, tkv):
    t, d = x2.shape
    tiles_per_seq = seq // tm
    hq = MLA_HEADS * LANES
    pad_in = jnp.zeros((d, LANES), F32).at[:, QK_NOPE:QK_NOPE + QK_ROPE].set(w_in[:, Q_LORA + KV_LORA:])
    w_in_p = jnp.concatenate([w_in[:, :Q_LORA + KV_LORA], pad_in], axis=1).astype(BF16)
    w_qb_p = jnp.pad(w_qb.reshape(Q_LORA, MLA_HEADS, QK_NOPE + QK_ROPE),
                     ((0, 0), (0, 0), (0, LANES - QK_NOPE - QK_ROPE))).reshape(Q_LORA, hq).astype(BF16)
    w_kv3 = w_kvb.reshape(KV_LORA, MLA_HEADS, QK_NOPE + V_HEAD)
    w_k_p = jnp.pad(w_kv3[:, :, :QK_NOPE], ((0, 0), (0, 0), (0, LANES - QK_NOPE))
                    ).reshape(KV_LORA, hq).astype(BF16)
    w_v = w_kv3[:, :, QK_NOPE:].reshape(KV_LORA, MLA_HEADS * V_HEAD).astype(BF16)
    half = QK_ROPE // 2
    inv_freq = ROPE_THETA ** (-jnp.arange(half, dtype=F32) / half)
    zeros = jnp.zeros((QK_NOPE,), F32)
    tail = jnp.zeros((LANES - QK_NOPE - QK_ROPE,), F32)
    freq = jnp.concatenate([zeros, inv_freq, inv_freq, tail]).reshape(1, LANES)
    sign = jnp.concatenate([zeros, -jnp.ones((half,), F32), jnp.ones((half,), F32), tail]).reshape(1, LANES)
    scale = float(QK_NOPE + QK_ROPE) ** -0.5 * math.log2(math.e)

    const = _resident
    return pl.pallas_call(
        functools.partial(_mla_proj_kernel, scale=scale),
        grid=(t // tm,),
        in_specs=[
            pl.BlockSpec((tm, d), lambda i: (i, 0)),
            pl.BlockSpec((tm, 1), lambda i: (i, 0)),
            const((1, d)),
            _mod_spec(layer, 1, batch, tiles_per_seq, d),
            _mod_spec(layer, 0, batch, tiles_per_seq, d),
            const(w_in_p.shape), const((1, Q_LORA)), const(w_qb_p.shape),
            const((1, KV_LORA)), const(w_k_p.shape), const(w_v.shape),
            const((1, LANES)), const((1, LANES)),
        ],
        out_specs=[
            pl.BlockSpec((tm, hq), lambda i: (i, 0)),
            pl.BlockSpec((tm, hq), lambda i: (i, 0)),
            pl.BlockSpec((tm // tkv, MLA_HEADS * V_HEAD, tkv), lambda i: (i, 0, 0)),
        ],
        out_shape=[
            jax.ShapeDtypeStruct((t, hq), BF16),
            jax.ShapeDtypeStruct((t, hq), BF16),
            jax.ShapeDtypeStruct((t // tkv, MLA_HEADS * V_HEAD, tkv), BF16),
        ],
        compiler_params=_params("parallel"),
        name="mla_proj",
    )(x2, pos, gain.reshape(1, d), mod3, mod3, w_in_p, q_norm.reshape(1, Q_LORA), w_qb_p,
      kv_norm.reshape(1, KV_LORA), w_k_p, w_v, freq, sign)


def _attn_kernel(q_ref, k_ref, vt_ref, o_ref, s_a, s_b, p_a, p_b, acc_ref, stat_ref, *, tq, n_heads):
    qi = pl.program_id(2)
    kv_idx = lax.broadcasted_iota(jnp.int32, (tq, tq), 0)
    q_idx = lax.broadcasted_iota(jnp.int32, (tq, tq), 1)
    causal = kv_idx <= q_idx
    heads = range(n_heads)

    def scores_into(j, s_ref):
        r0 = pl.multiple_of(j * tq, tq)
        for h in heads:
            q = q_ref[0, :, h * LANES:(h + 1) * LANES]
            s_ref[h] = _dot_nt(k_ref[0, pl.ds(r0, tq), h * LANES:(h + 1) * LANES], q)

    def values(j, p):
        return [_dot(vt_ref[j, h * V_HEAD:(h + 1) * V_HEAD, :], p(h)) for h in heads]

    def softmax(s_ref, masked):
        p_all = []
        for h in heads:
            s = s_ref[h]
            if masked:
                s = jnp.where(causal, s, MASK_VALUE)
            m_old = stat_ref[h, 0:1, :]
            m_new = jnp.maximum(m_old, jnp.max(s, axis=0, keepdims=True))
            p = jnp.exp2(s - m_new)
            alpha = jnp.exp2(m_old - m_new)
            stat_ref[h, 0:1, :] = m_new
            stat_ref[h, 1:2, :] = alpha * stat_ref[h, 1:2, :] + jnp.sum(p, axis=0, keepdims=True)
            stat_ref[h, 2:3, :] = alpha
            p_all.append(p.astype(BF16))
        return p_all

    def step(j, s_cur, p_cur, s_next, p_prev):
        pv = values(jnp.maximum(j - 1, 0), lambda h: p_prev[h])
        alpha_prev = [stat_ref[h, 2:3, :] for h in heads]
        scores_into(j + 1, s_next)
        p = softmax(s_cur, masked=False)
        for h in heads:
            p_cur[h] = p[h]
            acc_ref[h] = alpha_prev[h] * acc_ref[h] + pv[h]

    def finish(s_cur, p_prev):
        pv = values(jnp.maximum(qi - 1, 0), lambda h: p_prev[h])
        alpha_prev = [stat_ref[h, 2:3, :] for h in heads]
        p = softmax(s_cur, masked=True)
        pv_last = values(qi, lambda h: p[h])
        out = []
        for h in heads:
            acc = stat_ref[h, 2:3, :] * (alpha_prev[h] * acc_ref[h] + pv[h]) + pv_last[h]
            out.append(acc / stat_ref[h, 1:2, :])
        for pair in range(n_heads // 2):
            o_t = jnp.concatenate(out[2 * pair:2 * pair + 2], axis=0)
            o_ref[0, :, pair * LANES:(pair + 1) * LANES] = o_t.T.astype(BF16)

    row3 = lax.broadcasted_iota(jnp.int32, stat_ref.shape, 1)
    stat_ref[...] = jnp.where(row3 == 0, MASK_VALUE, jnp.where(row3 == 2, 1.0, 0.0))
    acc_ref[...] = jnp.zeros(acc_ref.shape, F32)
    p_b[...] = jnp.zeros(p_b.shape, BF16)
    scores_into(0, s_a)

    def pair_of_steps(jj, _):
        step(2 * jj, s_a, p_a, s_b, p_b)
        step(2 * jj + 1, s_b, p_b, s_a, p_a)
        return 0

    lax.fori_loop(0, qi // 2, pair_of_steps, 0)

    @pl.when(qi % 2 == 1)
    def _():
        step(qi - 1, s_a, p_a, s_b, p_b)
        finish(s_b, p_a)

    @pl.when(qi % 2 == 0)
    def _():
        finish(s_a, p_b)


def _mla_attention(q, k, vt, *, batch, seq, tq, n_heads):
    hq = MLA_HEADS * LANES
    q3 = q.reshape(batch, seq, hq)
    k3 = k.reshape(batch, seq, hq)
    n_kv = seq // tq
    out = pl.pallas_call(
        functools.partial(_attn_kernel, tq=tq, n_heads=n_heads),
        grid=(batch, MLA_HEADS // n_heads, seq // tq),
        in_specs=[
            pl.BlockSpec((1, tq, n_heads * LANES), lambda b, h, i: (b, i, h)),
            pl.BlockSpec((1, seq, n_heads * LANES), lambda b, h, i: (b, 0, h)),
            pl.BlockSpec((n_kv, n_heads * V_HEAD, tq), lambda b, h, i: (b, h, 0)),
        ],
        out_specs=pl.BlockSpec((1, tq, n_heads * V_HEAD), lambda b, h, i: (b, i, h)),
        out_shape=jax.ShapeDtypeStruct((batch, seq, MLA_HEADS * V_HEAD), BF16),
        scratch_shapes=[
            pltpu.VMEM((n_heads, tq, tq), F32), pltpu.VMEM((n_heads, tq, tq), F32),
            pltpu.VMEM((n_heads, tq, tq), BF16), pltpu.VMEM((n_heads, tq, tq), BF16),
            pltpu.VMEM((n_heads, V_HEAD, tq), F32), pltpu.VMEM((n_heads, 8, tq), F32),
        ],
        compiler_params=_params("parallel", "parallel", "parallel"),
        name="mla_attn",
    )(q3, k3, vt)
    return out.reshape(batch * seq, MLA_HEADS * V_HEAD)


def _swiglu(h, wg_ref, wu_ref, wd_ref, n_chunks):
    f = wd_ref.shape[0]
    tf = f // n_chunks
    acc = None
    for j in range(n_chunks):
        sl = slice(j * tf, (j + 1) * tf)
        g = _dot(h, wg_ref[:, sl])
        u = _dot(h, wu_ref[:, sl])
        part = _dot((_silu(g) * u).astype(BF16), wd_ref[sl, :])
        acc = part if acc is None else acc + part
    return acc


def _mixer_ffn_kernel(a_ref, wo_ref, gate1_ref, x_ref, gain_ref, sc_ref, sh_ref, gate2_ref,
                      wg_ref, wu_ref, wd_ref, o_ref, *, n_chunks):
    x = x_ref[...] + gate1_ref[...] * _dot(a_ref[...], wo_ref[...])
    h = _mod_norm(x, gain_ref[...], sc_ref[...], sh_ref[...]).astype(BF16)
    o_ref[...] = x + gate2_ref[...] * _swiglu(h, wg_ref, wu_ref, wd_ref, n_chunks)


def _mixer_ffn(a, w_out, x2, mod3, gain, w_gate_up, w_down, *, layer, batch, seq, tm):
    t, d = x2.shape
    f = w_down.shape[0]
    kdim = a.shape[1]
    tiles_per_seq = seq // tm
    wgu = w_gate_up.astype(BF16)
    row_spec = lambda n: pl.BlockSpec((tm, n), lambda i: (i, 0))
    return pl.pallas_call(
        functools.partial(_mixer_ffn_kernel, n_chunks=2),
        grid=(t // tm,),
        in_specs=[
            row_spec(kdim), _resident((kdim, d)),
            _mod_spec(layer, 2, batch, tiles_per_seq, d),
            row_spec(d), _resident((1, d)),
            _mod_spec(layer, 4, batch, tiles_per_seq, d),
            _mod_spec(layer, 3, batch, tiles_per_seq, d),
            _mod_spec(layer, 5, batch, tiles_per_seq, d),
            _resident((d, f)),
            _resident((d, f), (0, 1)),
            _resident((f, d)),
        ],
        out_specs=row_spec(d),
        out_shape=jax.ShapeDtypeStruct((t, d), F32),
        compiler_params=_params("parallel"),
        name="mixer_ffn",
    )(a, w_out.astype(BF16), mod3, x2, gain.reshape(1, d), mod3, mod3, mod3, wgu, wgu,
      w_down.astype(BF16))


def _gdn_proj_kernel(x_ref, gain_ref, sc_ref, sh_ref, w_ref, conv_ref, alog_ref, dtb_ref,
                     q_out, k_out, v_out, z_out, bg_out, tail_ref, *, tiles_per_seq):
    tm = x_ref.shape[0]
    nqk = GDN_HEADS * GDN_DK
    nqkv = 2 * nqk + GDN_HEADS * GDN_DV
    nz = GDN_HEADS * GDN_DV
    i = pl.program_id(0)
    h = _mod_norm(x_ref[...], gain_ref[...], sc_ref[...], sh_ref[...]).astype(BF16)

    @pl.when(i % tiles_per_seq == 0)
    def _():
        tail_ref[...] = jnp.zeros(tail_ref.shape, F32)

    def l2n(t):
        return t * lax.rsqrt(jnp.sum(t * t, axis=-1, keepdims=True) + EPS)

    for c0 in range(0, nqkv, GDN_PROJ_COLS):
        cols = slice(c0, c0 + GDN_PROJ_COLS)
        cur = _dot(h, w_ref[:, cols])
        ext = jnp.concatenate([tail_ref[:, cols], cur], axis=0)
        tail_ref[:, cols] = cur[tm - 8:, :]
        y = conv_ref[CONV_K - 1:CONV_K, cols] * cur
        for back in range(1, CONV_K):
            tap = CONV_K - 1 - back
            y = y + conv_ref[tap:tap + 1, cols] * pltpu.roll(ext, back, 0)[8:, :]
        act = _silu(y)
        for h0 in range(0, GDN_PROJ_COLS, GDN_DK):
            head = act[:, h0:h0 + GDN_DK]
            dst = slice(c0 % nqk + h0, c0 % nqk + h0 + GDN_DK)
            if c0 < nqk:
                q_out[:, dst] = (l2n(head) * (GDN_DK ** -0.5)).astype(BF16)
            elif c0 < 2 * nqk:
                k_out[:, dst] = l2n(head).astype(BF16)
            else:
                v_out[:, dst] = head.astype(BF16)
    for c0 in range(0, nz, GDN_PROJ_COLS):
        z_out[:, c0:c0 + GDN_PROJ_COLS] = _dot(h, w_ref[:, nqkv + c0:nqkv + c0 + GDN_PROJ_COLS]).astype(BF16)

    ba = _dot(h, w_ref[:, nqkv + nz:])
    lane = lax.broadcasted_iota(jnp.int32, ba.shape, 1) & (LANES - 1)
    beta = jax.nn.sigmoid(ba)
    pre = ba + dtb_ref[...]
    softplus = jnp.maximum(pre, 0.0) + jnp.log(1.0 + jnp.exp(-jnp.abs(pre)))
    g = jnp.where((lane >= GDN_GROUP) & (lane < 2 * GDN_GROUP), -jnp.exp(alog_ref[...]) * softplus, 0.0)
    row = lax.broadcasted_iota(jnp.int32, (tm, tm), 0)
    col = lax.broadcasted_iota(jnp.int32, (tm, tm), 1)
    same_chunk = (row & -CHUNK) == (col & -CHUNK)
    tril = jnp.where(same_chunk & (col <= row), 1.0, 0.0).astype(BF16)
    g_hi, g_mid, g_lo = _split3(g)
    gc = _dot(tril, g_hi) + (_dot(tril, g_mid) + _dot(tril, g_lo))
    bg_out[...] = jnp.where(lane < GDN_GROUP, beta, gc)


def _gdn_proj(x2, mod3, gain, w_in, conv_w, a_log, dt_bias, *, layer, batch, seq, tm):
    t, d = x2.shape
    nqk = GDN_HEADS * GDN_DK
    nqkv = 2 * nqk + GDN_HEADS * GDN_DV
    nz = GDN_HEADS * GDN_DV
    n_groups = GDN_HEADS // GDN_GROUP
    tiles_per_seq = seq // tm
    wb = w_in[:, nqkv + nz:nqkv + nz + GDN_HEADS]
    wa = w_in[:, nqkv + nz + GDN_HEADS:]

    def group_rows(b_part, a_part, dtype):
        rows = b_part.shape[0]
        chunks = []
        for gidx in range(n_groups):
            sl = slice(gidx * GDN_GROUP, (gidx + 1) * GDN_GROUP)
            chunks += [b_part[:, sl], a_part[:, sl], jnp.zeros((rows, LANES - 2 * GDN_GROUP), dtype)]
        return jnp.concatenate(chunks, axis=1)

    w_p = jnp.concatenate([w_in[:, :nqkv + nz], group_rows(wb, wa, F32)], axis=1).astype(BF16)
    zero_h = jnp.zeros((1, GDN_HEADS), F32)
    alog_p = group_rows(zero_h, a_log.reshape(1, GDN_HEADS), F32)
    dtb_p = group_rows(zero_h, dt_bias.reshape(1, GDN_HEADS), F32)
    const = _resident
    row_spec = lambda n: pl.BlockSpec((tm, n), lambda i: (i, 0))
    return pl.pallas_call(
        functools.partial(_gdn_proj_kernel, tiles_per_seq=tiles_per_seq),
        grid=(t // tm,),
        in_specs=[
            row_spec(d), const((1, d)),
            _mod_spec(layer, 1, batch, tiles_per_seq, d),
            _mod_spec(layer, 0, batch, tiles_per_seq, d),
            const(w_p.shape), const((CONV_K, nqkv)),
            const((1, n_groups * LANES)), const((1, n_groups * LANES)),
        ],
        out_specs=[row_spec(nqk), row_spec(nqk), row_spec(nz), row_spec(nz), row_spec(n_groups * LANES)],
        out_shape=[
            jax.ShapeDtypeStruct((t, nqk), BF16),
            jax.ShapeDtypeStruct((t, nqk), BF16),
            jax.ShapeDtypeStruct((t, nz), BF16),
            jax.ShapeDtypeStruct((t, nz), BF16),
            jax.ShapeDtypeStruct((t, n_groups * LANES), F32),
        ],
        scratch_shapes=[pltpu.VMEM((8, nqkv), F32)],
        compiler_params=_params("arbitrary"),
        name="gdn_proj",
    )(x2, gain.reshape(1, d), mod3, mod3, w_p, conv_w, alog_p, dtb_p)


def _gdn_core_kernel(q_ref, k_ref, v_ref, z_ref, bg_ref, gn_ref, o_ref, state_ref):
    ts = q_ref.shape[1]
    c = CHUNK
    nc = ts // c
    nh = GDN_HEADS
    row = lax.broadcasted_iota(jnp.int32, (1, c, c), 1)
    col = lax.broadcasted_iota(jnp.int32, (1, c, c), 2)
    causal = col <= row
    strict = col < row
    eye = jnp.where(row == col, 1.0, 0.0).astype(F32)

    def level_mask(s):
        return ((row & -2 * s) == (col & -2 * s)) & ((row & s) != 0) & ((col & s) == 0)

    lane = lax.broadcasted_iota(jnp.int32, (nh * nc, c, LANES), 2)

    @pl.when(pl.program_id(1) == 0)
    def _():
        state_ref[...] = jnp.zeros(state_ref.shape, F32)

    bg = bg_ref[0]

    def stack(ref):
        return jnp.concatenate(
            [ref[0, :, hd * GDN_DK:(hd + 1) * GDN_DK].reshape(nc, c, GDN_DK) for hd in range(nh)], axis=0)

    def stack_col(first_lane):
        return jnp.concatenate(
            [bg[:, first_lane + hd:first_lane + hd + 1].reshape(nc, c, 1) for hd in range(nh)], axis=0)

    q = stack(q_ref).astype(F32)
    k16 = stack(k_ref)
    k = k16.astype(F32)
    v = stack(v_ref).astype(F32)
    beta = stack_col(0)
    gc = stack_col(nh)

    g_hi, g_mid, g_lo = (t.astype(F32) for t in _split3(gc))
    terms = lambda base, other: jnp.where(lane == base, g_hi, jnp.where(
        lane == base + 1, g_mid, jnp.where(lane == base + 2, g_lo, other)))
    lhs = terms(0, jnp.where(lane < 6, 1.0, 0.0)).astype(BF16)
    rhs = terms(3, jnp.where(lane < 3, -1.0, 0.0)).astype(BF16)
    diff = -_bmm_nt(lhs, rhs)
    decay = jnp.where(causal, jnp.exp(jnp.where(causal, diff, 0.0)), 0.0)

    kb = k * beta
    a_mat = jnp.where(strict, _bmm_nt(kb.astype(BF16), k16) * decay, 0.0)
    attn = jnp.where(causal, _bmm_nt(q.astype(BF16), k16) * decay, 0.0).astype(BF16)

    inv = eye - jnp.where(level_mask(1), a_mat, 0.0)
    s = 2
    while s < c:
        x_s = jnp.where(level_mask(s), a_mat, 0.0).astype(BF16)
        inv16 = inv.astype(BF16)
        inv = inv - _bmm(_bmm(inv16, x_s).astype(BF16), inv16)
        s *= 2

    eg = jnp.exp(gc)
    g_last = gc[:, c - 1:c, :]
    rhs_uw = jnp.concatenate([v * beta, kb * eg], axis=-1).astype(BF16)
    uw = _bmm(inv.astype(BF16), rhs_uw)
    a_uw = _bmm(attn, uw.astype(BF16))
    o_intra = a_uw[..., :GDN_DV]
    q_eff = (q * eg - a_uw[..., GDN_DV:]).astype(BF16)
    kt_uw = _bmm_tn(k16, (uw * jnp.exp(g_last - gc)).astype(BF16))
    s_add = kt_uw[..., :GDN_DV]
    s_mul = kt_uw[..., GDN_DV:].astype(BF16)
    dec = jnp.exp(g_last)

    per_head = lambda a: a.reshape((nh, nc) + a.shape[1:])
    q_eff, o_intra, s_add, s_mul, dec = map(per_head, (q_eff, o_intra, s_add, s_mul, dec))
    st = state_ref[...]
    outs = []
    for n in range(nc):
        st16 = st.astype(BF16)
        outs.append(_bmm(q_eff[:, n], st16) + o_intra[:, n])
        st = dec[:, n] * st - _bmm(s_mul[:, n], st16) + s_add[:, n]
    state_ref[...] = st
    o = jnp.stack(outs, axis=1).reshape(nh, ts, GDN_DV)
    gn = gn_ref[...]
    for hd in range(nh):
        sl = slice(hd * GDN_DV, (hd + 1) * GDN_DV)
        z = z_ref[0, :, sl].astype(F32)
        o_ref[0, :, sl] = (_rms(o[hd]) * gn * _silu(z)).astype(BF16)


def _gdn_core(q, k, v, z, bg, out_norm, *, batch, seq, ts):
    width = GDN_HEADS * GDN_DK
    shp = lambda a: a.reshape(batch, seq, a.shape[-1])
    blk = pl.BlockSpec((1, ts, width), lambda b, i: (b, i, 0))
    out = pl.pallas_call(
        _gdn_core_kernel,
        grid=(batch, seq // ts),
        in_specs=[blk, blk, blk, blk,
                  pl.BlockSpec((1, ts, LANES), lambda b, i: (b, i, 0)),
                  pl.BlockSpec((1, GDN_DV), lambda b, i: (0, 0))],
        out_specs=blk,
        out_shape=jax.ShapeDtypeStruct((batch, seq, GDN_HEADS * GDN_DV), BF16),
        scratch_shapes=[pltpu.VMEM((GDN_HEADS, GDN_DK, GDN_DV), F32)],
        compiler_params=_params("parallel", "arbitrary"),
        name="gdn_core",
    )(shp(q), shp(k), shp(v), shp(z), shp(bg), out_norm.reshape(1, GDN_DV))
    return out.reshape(batch * seq, GDN_HEADS * GDN_DV)


def _moe_slots(tb):
    slots = MOE_TOP_K * tb + N_EXPERTS * MOE_GRANULE
    return slots, slots // MOE_GRANULE


def _moe_route_kernel(a_ref, wo_ref, gate1_ref, x_ref, gain_ref, sc_ref, sh_ref, wr_ref,
                      x_out, hs_ref, rt_ref, meta_ref):
    tb = x_ref.shape[0]
    slots = hs_ref.shape[0]
    x = x_ref[...] + gate1_ref[...] * _dot(a_ref[...], wo_ref[...])
    x_out[...] = x
    h = _mod_norm(x, gain_ref[...], sc_ref[...], sh_ref[...])
    h_hi, h_mid, _ = _split3(h)
    logits = _dot(h_hi, wr_ref[0]) + (_dot(h_hi, wr_ref[1]) + _dot(h_mid, wr_ref[0]))
    lane = lax.broadcasted_iota(jnp.int32, (tb, LANES), 1)
    lane_f = lane.astype(F32)
    logits = jnp.where(lane < N_EXPERTS, logits, -jnp.inf)
    m1 = jnp.max(logits, axis=-1, keepdims=True)
    i1 = jnp.min(jnp.where(logits == m1, lane_f, float(LANES)), axis=-1, keepdims=True)
    rest = jnp.where(lane_f == i1, -jnp.inf, logits)
    m2 = jnp.max(rest, axis=-1, keepdims=True)
    i2 = jnp.min(jnp.where(rest == m2, lane_f, float(LANES)), axis=-1, keepdims=True)
    e2 = jnp.exp(m2 - m1)
    w1 = 1.0 / (1.0 + e2)
    w2 = e2 * w1

    first = lane_f == i1
    second = lane_f == i2
    chosen = jnp.where(first, 1.0, jnp.where(second, 1.0, 0.0))
    r = lax.broadcasted_iota(jnp.int32, (tb, tb), 0)
    c = lax.broadcasted_iota(jnp.int32, (tb, tb), 1)
    earlier = jnp.where(c < r, 1.0, 0.0).astype(BF16)
    rank = _dot(earlier, chosen.astype(BF16))
    count = jnp.sum(chosen, axis=0, keepdims=True)
    granules = jnp.floor((count + (MOE_GRANULE - 1)) * (1.0 / MOE_GRANULE))
    er = lax.broadcasted_iota(jnp.int32, (LANES, LANES), 0)
    ec = lax.broadcasted_iota(jnp.int32, (LANES, LANES), 1)
    before = jnp.where(er < ec, 1.0, 0.0).astype(BF16)
    seg_gran = _dot(jnp.broadcast_to(granules, (8, LANES)).astype(BF16), before)[0:1]
    slot_of = seg_gran * MOE_GRANULE + rank
    pos1 = jnp.sum(jnp.where(first, slot_of, 0.0), axis=-1, keepdims=True)
    pos2 = jnp.sum(jnp.where(second, slot_of, 0.0), axis=-1, keepdims=True)

    slot = lax.broadcasted_iota(jnp.int32, (tb, slots), 1).astype(F32)
    place = jnp.where(slot == pos1, 1.0, jnp.where(slot == pos2, 1.0, 0.0)).astype(BF16)
    hs_ref[...] = _dot_tn(place, h_hi).astype(BF16)
    rt_ref[...] = jnp.where(lane == 0, pos1, jnp.where(lane == 1, pos2, jnp.where(
        lane == 2, w1, jnp.where(lane == 3, w2, 0.0))))
    row8 = lax.broadcasted_iota(jnp.int32, (8, LANES), 0)
    meta_ref[0] = jnp.where(row8 == 0, granules, jnp.where(row8 == 1, seg_gran, 0.0))


def _moe_expert_kernel(src_ref, te_ref, used_ref, *refs):
    n_gran = len(refs) - 4
    wg_ref, wu_ref, wd_ref, o_ref = refs[n_gran:]
    step = pl.program_id(0)

    @pl.when(step < used_ref[0])
    def _():
        h = jnp.concatenate([g[...] for g in refs[:n_gran]], axis=0)
        o_ref[...] = _swiglu(h, wg_ref.at[0], wu_ref.at[0], wd_ref.at[0], 1).astype(BF16)

    @pl.when(step >= used_ref[0])
    def _():
        o_ref[...] = jnp.zeros(o_ref.shape, BF16)


def _moe_combine_kernel(inv_ref, *refs):
    n_gran = len(refs) - 5
    rt_ref, x_ref, gate_ref, fn_ref, o_ref = refs[n_gran:]
    tb = x_ref.shape[0]
    ys = jnp.concatenate([g[...] for g in refs[:n_gran]], axis=0)
    rt = rt_ref[...]
    pos1, pos2, w1, w2 = rt[:, 0:1], rt[:, 1:2], rt[:, 2:3], rt[:, 3:4]
    slot = lax.broadcasted_iota(jnp.int32, (tb, ys.shape[0]), 1).astype(F32)
    y1 = _dot(jnp.where(slot == pos1, 1.0, 0.0).astype(BF16), ys)
    y2 = _dot(jnp.where(slot == pos2, 1.0, 0.0).astype(BF16), ys)
    xo = x_ref[...] + gate_ref[...] * (w1 * y1 + w2 * y2)
    o_ref[...] = _rms(xo) * fn_ref[...]


def _moe_tables(meta, n_steps, gran_per_step, gran_per_tile):
    i32 = jnp.int32
    cnt = meta[:, 0, :N_EXPERTS].astype(i32)
    seg = meta[:, 1, :N_EXPERTS].astype(i32)
    n_tiles = cnt.shape[0]
    earlier = jnp.cumsum(cnt, axis=0) - cnt
    steps = (jnp.sum(cnt, axis=0) + gran_per_step - 1) // gran_per_step
    step_end = jnp.cumsum(steps)
    used = step_end[-1:]
    seg_pos = (step_end - steps)[None, :] * gran_per_step + earlier
    seg_src = jnp.arange(n_tiles, dtype=i32)[:, None] * gran_per_tile + seg

    s = jnp.arange(n_steps * gran_per_step, dtype=i32)[:, None]
    pos, length, start = (a.reshape(1, -1) for a in (seg_pos, cnt, seg_src))
    src = jnp.sum(jnp.where((s >= pos) & (s < pos + length), start - pos + s, 0), axis=1)
    st = jnp.arange(n_steps, dtype=i32)[:, None]
    step_expert = jnp.minimum(jnp.sum((step_end[None, :] <= st).astype(i32), axis=1), N_EXPERTS - 1)

    g = jnp.arange(gran_per_tile, dtype=i32)[None, :, None]
    lo, n, base = (a[:, None, :] for a in (seg, cnt, seg_pos))
    inv = jnp.sum(jnp.where((g >= lo) & (g < lo + n), base + g - lo, 0), axis=-1).reshape(-1)
    return src, step_expert, used, inv


def _mixer_moe(a, w_out, x2, mod3, gain, w_router, w_gate_up, w_down, final_norm, *, layer, batch, seq, tb):
    t, d = x2.shape
    kdim = a.shape[1]
    n_e, fe, _ = w_down.shape
    tiles_per_seq = seq // tb
    n_tiles = t // tb
    slots, gran_per_tile = _moe_slots(tb)
    gran_per_step = MOE_LHS_ROWS // MOE_GRANULE
    n_steps = pl.cdiv(MOE_TOP_K * t // MOE_GRANULE + n_tiles * n_e, gran_per_step) + n_e
    wr = jnp.pad(w_router, ((0, 0), (0, LANES - n_e)))
    wr_hi = wr.astype(BF16)
    wr_lo = (wr - wr_hi.astype(F32)).astype(BF16)
    wr2 = jnp.stack([wr_hi, wr_lo])
    wgu = w_gate_up.astype(BF16)
    row_spec = lambda rows, cols: pl.BlockSpec((rows, cols), lambda i, *_: (i, 0))

    x2, hs, rt, meta = pl.pallas_call(
        _moe_route_kernel,
        grid=(n_tiles,),
        in_specs=[
            row_spec(tb, kdim), _resident((kdim, d)),
            _mod_spec(layer, 2, batch, tiles_per_seq, d),
            row_spec(tb, d), _resident((1, d)),
            _mod_spec(layer, 4, batch, tiles_per_seq, d),
            _mod_spec(layer, 3, batch, tiles_per_seq, d),
            _resident((2, d, LANES)),
        ],
        out_specs=[row_spec(tb, d), row_spec(slots, d), row_spec(tb, LANES),
                   pl.BlockSpec((1, 8, LANES), lambda i: (i, 0, 0))],
        out_shape=[
            jax.ShapeDtypeStruct((t, d), F32),
            jax.ShapeDtypeStruct((n_tiles * slots, d), BF16),
            jax.ShapeDtypeStruct((t, LANES), F32),
            jax.ShapeDtypeStruct((n_tiles, 8, LANES), F32),
        ],
        compiler_params=_params("parallel"),
        name="moe_route",
    )(a, w_out.astype(BF16), mod3, x2, gain.reshape(1, d), mod3, mod3, wr2)

    src, step_expert, used, inv = _moe_tables(meta, n_steps, gran_per_step, gran_per_tile)

    def granule_spec(j, per_step):
        return pl.BlockSpec((MOE_GRANULE, d), lambda i, tbl, *_: (tbl[i * per_step + j], 0))

    ye = pl.pallas_call(
        _moe_expert_kernel,
        grid_spec=pltpu.PrefetchScalarGridSpec(
            num_scalar_prefetch=3,
            grid=(n_steps,),
            in_specs=[granule_spec(j, gran_per_step) for j in range(gran_per_step)] + [
                pl.BlockSpec((1, d, fe), lambda i, src, te, used: (te[i], 0, 0)),
                pl.BlockSpec((1, d, fe), lambda i, src, te, used: (te[i], 0, 1)),
                pl.BlockSpec((1, fe, d), lambda i, src, te, used: (te[i], 0, 0)),
            ],
            out_specs=pl.BlockSpec((MOE_LHS_ROWS, d), lambda i, *_: (i, 0)),
        ),
        out_shape=jax.ShapeDtypeStruct((n_steps * MOE_LHS_ROWS, d), BF16),
        compiler_params=_params("arbitrary"),
        name="moe_experts",
    )(src, step_expert, used, *([hs] * gran_per_step), wgu, wgu, w_down.astype(BF16))

    return pl.pallas_call(
        _moe_combine_kernel,
        grid_spec=pltpu.PrefetchScalarGridSpec(
            num_scalar_prefetch=1,
            grid=(n_tiles,),
            in_specs=[granule_spec(j, gran_per_tile) for j in range(gran_per_tile)] + [
                row_spec(tb, LANES), row_spec(tb, d),
                _mod_spec(layer, 5, batch, tiles_per_seq, d),
                pl.BlockSpec((1, d), lambda i, *_: (0, 0)),
            ],
            out_specs=row_spec(tb, d),
        ),
        out_shape=jax.ShapeDtypeStruct((t, d), F32),
        compiler_params=_params("parallel"),
        name="moe_combine",
    )(inv, *([ye] * gran_per_tile), rt, x2, mod3, final_norm.reshape(1, d))


def kernel(x, c, positions, ada_w, ada_b, norm_mix, norm_ffn, mla_w_in, mla_q_norm, mla_w_qb, mla_kv_norm, mla_w_kvb, mla_w_out, ffn_w_gate_up, ffn_w_down, gdn_w_in, gdn_conv_w, gdn_a_log, gdn_dt_bias, gdn_out_norm, gdn_w_out, moe_w_router, moe_w_gate_up, moe_w_down, final_norm):
    batch, seq, d = x.shape
    depth = ada_w.shape[0]
    assert depth == 2 and seq % 512 == 0
    t = batch * seq
    tm = 512
    dims = dict(batch=batch, seq=seq, tm=tm)

    mod = _adaln(c, ada_w, ada_b)
    mod3 = mod.reshape(depth * batch * N_MOD, 1, d)
    x2 = x.reshape(t, d)
    pos = positions.astype(F32).reshape(t, 1)

    tq = 256
    q, k, vt = _mla_proj(x2, pos, mod3, norm_mix[0], mla_w_in[0], mla_q_norm[0], mla_w_qb[0],
                         mla_kv_norm[0], mla_w_kvb[0], layer=0, tkv=tq, **dims)
    attn = _mla_attention(q, k, vt, batch=batch, seq=seq, tq=tq, n_heads=8)
    x2 = _mixer_ffn(attn, mla_w_out[0], x2, mod3, norm_ffn[0], ffn_w_gate_up[0], ffn_w_down[0],
                    layer=0, **dims)

    gq, gk, gv, gz, bg = _gdn_proj(x2, mod3, norm_mix[1], gdn_w_in[0], gdn_conv_w[0], gdn_a_log[0],
                                   gdn_dt_bias[0], layer=1, **dims)
    og = _gdn_core(gq, gk, gv, gz, bg, gdn_out_norm[0], batch=batch, seq=seq, ts=256)
    out = _mixer_moe(og, gdn_w_out[0], x2, mod3, norm_ffn[1], moe_w_router[0], moe_w_gate_up[0],
                     moe_w_down[0], final_norm, layer=1, batch=batch, seq=seq, tb=256)
    return out.reshape(batch, seq, d)
```

```python
import functools
import math

import jax
import jax.numpy as jnp
from jax import lax
from jax.experimental import pallas as pl
from jax.experimental.pallas import tpu as pltpu

F32 = jnp.float32
BF16 = jnp.bfloat16

EPS = 1e-6
N_MOD = 6
LANES = 128
VMEM_LIMIT_BYTES = 60 * 1024 * 1024

MLA_HEADS = 16
Q_LORA = 512
KV_LORA = 256
QK_NOPE = 64
QK_ROPE = 32
V_HEAD = 64
ROPE_THETA = 10000.0
MASK_VALUE = -1e30
GDN_HEADS = 8
GDN_DK = 128
GDN_DV = 128
CONV_K = 4
CHUNK = 64
GDN_GROUP = GDN_HEADS
GDN_PROJ_COLS = 256
N_EXPERTS = 8
MOE_TOP_K = 2
MOE_GRANULE = 16
MOE_LHS_ROWS = 512


def _params(*sem):
    return pltpu.CompilerParams(dimension_semantics=sem, vmem_limit_bytes=VMEM_LIMIT_BYTES)


def _resident(shape, index=None):
    index = (0,) * len(shape) if index is None else index
    return pl.BlockSpec(shape, lambda *_: index, pipeline_mode=pl.Buffered(1))


def _split3(a):
    hi = a.astype(BF16)
    r = a - hi.astype(F32)
    mid = r.astype(BF16)
    lo = (r - mid.astype(F32)).astype(BF16)
    return hi, mid, lo


def _dot(a, b):
    return jnp.dot(a, b, preferred_element_type=F32)


def _dot_nt(a, b):
    return lax.dot_general(a, b, (((1,), (1,)), ((), ())), preferred_element_type=F32)


def _dot_tn(a, b):
    return lax.dot_general(a, b, (((0,), (0,)), ((), ())), preferred_element_type=F32)


def _bmm(a, b):
    return lax.dot_general(a, b, (((2,), (1,)), ((0,), (0,))), preferred_element_type=F32)


def _bmm_nt(a, b):
    return lax.dot_general(a, b, (((2,), (2,)), ((0,), (0,))), preferred_element_type=F32)


def _bmm_tn(a, b):
    return lax.dot_general(a, b, (((1,), (1,)), ((0,), (0,))), preferred_element_type=F32)


def _dot_f32(a, b):
    a_hi, a_mid, _ = _split3(a)
    b_hi, b_mid, _ = _split3(b)
    return _dot(a_hi, b_hi) + (_dot(a_hi, b_mid) + _dot(a_mid, b_hi))


def _rms(x):
    return x * lax.rsqrt(jnp.mean(x * x, axis=-1, keepdims=True) + EPS)


def _silu(x):
    return x * jax.nn.sigmoid(x)


def _mod_norm(x, gain, sc, sh):
    return _rms(x) * gain * (1.0 + sc) + sh


def _adaln_kernel(c_ref, w_ref, b_ref, o_ref):
    c = c_ref[...]
    cond = _silu(c)
    o_ref[0] = _dot_f32(cond, w_ref[0]) + b_ref[0]


def _adaln(c, ada_w, ada_b):
    depth, d, n = ada_w.shape
    b = c.shape[0]
    tn = d
    return pl.pallas_call(
        _adaln_kernel,
        grid=(depth, n // tn),
        in_specs=[
            pl.BlockSpec((b, d), lambda l, j: (0, 0)),
            pl.BlockSpec((1, d, tn), lambda l, j: (l, 0, j)),
            pl.BlockSpec((1, 1, tn), lambda l, j: (l, 0, j)),
        ],
        out_specs=pl.BlockSpec((1, b, tn), lambda l, j: (l, 0, j)),
        out_shape=jax.ShapeDtypeStruct((depth, b, n), F32),
        compiler_params=_params("parallel", "parallel"),
        name="adaln",
    )(c, ada_w, ada_b.reshape(depth, 1, n))


def _mod_spec(layer, which, batch, tiles_per_seq, d, tile_of=lambda i: i):
    def index(i, *_):
        return ((layer * batch + tile_of(i) // tiles_per_seq) * N_MOD + which, 0, 0)
    return pl.BlockSpec((None, 1, d), index)


def _mla_proj_kernel(x_ref, pos_ref, gain_ref, sc_ref, sh_ref, win_ref, qn_ref, wqb_ref,
                     kvn_ref, wk_ref, wv_ref, freq_ref, sign_ref, q_out, k_out, vt_out, *, scale):
    tm = x_ref.shape[0]
    h = _mod_norm(x_ref[...], gain_ref[...], sc_ref[...], sh_ref[...]).astype(BF16)
    proj = _dot(h, win_ref[...])
    q_lat = proj[:, :Q_LORA]
    kv_lat = proj[:, Q_LORA:Q_LORA + KV_LORA]
    k_rope = proj[:, Q_LORA + KV_LORA:]
    qn = (_rms(q_lat) * qn_ref[...]).astype(BF16)
    kvn = (_rms(kv_lat) * kvn_ref[...]).astype(BF16)

    ang = pos_ref[...] * freq_ref[...]
    cos = jnp.cos(ang)
    sin = jnp.sin(ang) * sign_ref[...]
    lane = lax.broadcasted_iota(jnp.int32, (tm, LANES), 1)
    first_half = lane < QK_NOPE + QK_ROPE // 2

    def rope(t, cos, sin):
        partner = jnp.where(first_half, pltpu.roll(t, LANES - QK_ROPE // 2, 1),
                            pltpu.roll(t, QK_ROPE // 2, 1))
        return t * cos + partner * sin

    q = _dot(qn, wqb_ref[...])
    k = _dot(kvn, wk_ref[...])
    k_rope = rope(k_rope, cos, sin)
    cos_q = cos * scale
    sin_q = sin * scale
    for hd in range(MLA_HEADS):
        sl = slice(hd * LANES, (hd + 1) * LANES)
        q_out[:, sl] = rope(q[:, sl], cos_q, sin_q).astype(BF16)
        k_out[:, sl] = (k[:, sl] + k_rope).astype(BF16)
    vt = _dot(kvn, wv_ref[...]).T
    tkv = vt_out.shape[2]
    for j in range(vt_out.shape[0]):
        vt_out[j] = vt[:, j * tkv:(j + 1) * tkv].astype(BF16)


def _mla_proj(x2, pos, mod3, gain, w_in, q_norm, w_qb, kv_norm, w_kvb, *, layer, batch, seq, tm, tkv):
    t, d = x2.shape
    tiles_per_seq = seq // tm
    hq = MLA_HEADS * LANES
    pad_in = jnp.zeros((d, LANES), F32).at[:, QK_NOPE:QK_NOPE + QK_ROPE].set(w_in[:, Q_LORA + KV_LORA:])
    w_in_p = jnp.concatenate([w_in[:, :Q_LORA + KV_LORA], pad_in], axis=1).astype(BF16)
    w_qb_p = jnp.pad(w_qb.reshape(Q_LORA, MLA_HEADS, QK_NOPE + QK_ROPE),
                     ((0, 0), (0, 0), (0, LANES - QK_NOPE - QK_ROPE))).reshape(Q_LORA, hq).astype(BF16)
    w_kv3 = w_kvb.reshape(KV_LORA, MLA_HEADS, QK_NOPE + V_HEAD)
    w_k_p = jnp.pad(w_kv3[:, :, :QK_NOPE], ((0, 0), (0, 0), (0, LANES - QK_NOPE))
                    ).reshape(KV_LORA, hq).astype(BF16)
    w_v = w_kv3[:, :, QK_NOPE:].reshape(KV_LORA, MLA_HEADS * V_HEAD).astype(BF16)
    half = QK_ROPE // 2
    inv_freq = ROPE_THETA ** (-jnp.arange(half, dtype=F32) / half)
    zeros = jnp.zeros((QK_NOPE,), F32)
    tail = jnp.zeros((LANES - QK_NOPE - QK_ROPE,), F32)
    freq = jnp.concatenate([zeros, inv_freq, inv_freq, tail]).reshape(1, LANES)
    sign = jnp.concatenate([zeros, -jnp.ones((half,), F32), jnp.ones((half,), F32), tail]).reshape(1, LANES)
    scale = float(QK_NOPE + QK_ROPE) ** -0.5 * math.log2(math.e)

    const = _resident
    return pl.pallas_call(
        functools.partial(_mla_proj_kernel, scale=scale),
        grid=(t // tm,),
        in_specs=[
            pl.BlockSpec((tm, d), lambda i: (i, 0)),
            pl.BlockSpec((tm, 1), lambda i: (i, 0)),
            const((1, d)),
            _mod_spec(layer, 1, batch, tiles_per_seq, d),
            _mod_spec(layer, 0, batch, tiles_per_seq, d),
            const(w_in_p.shape), const((1, Q_LORA)), const(w_qb_p.shape),
            const((1, KV_LORA)), const(w_k_p.shape), const(w_v.shape),
            const((1, LANES)), const((1, LANES)),
        ],
        out_specs=[
            pl.BlockSpec((tm, hq), lambda i: (i, 0)),
            pl.BlockSpec((tm, hq), lambda i: (i, 0)),
            pl.BlockSpec((tm // tkv, MLA_HEADS * V_HEAD, tkv), lambda i: (i, 0, 0)),
        ],
        out_shape=[
            jax.ShapeDtypeStruct((t, hq), BF16),
            jax.ShapeDtypeStruct((t, hq), BF16),
            jax.ShapeDtypeStruct((t // tkv, MLA_HEADS * V_HEAD, tkv), BF16),
        ],
        compiler_params=_params("parallel"),
        name="mla_proj",
    )(x2, pos, gain.reshape(1, d), mod3, mod3, w_in_p, q_norm.reshape(1, Q_LORA), w_qb_p,
      kv_norm.reshape(1, KV_LORA), w_k_p, w_v, freq, sign)


def _attn_kernel(q_ref, k_ref, vt_ref, o_ref, s_a, s_b, p_a, p_b, acc_ref, stat_ref, *, tq, n_heads):
    qi = pl.program_id(2)
    kv_idx = lax.broadcasted_iota(jnp.int32, (tq, tq), 0)
    q_idx = lax.broadcasted_iota(jnp.int32, (tq, tq), 1)
    causal = kv_idx <= q_idx
    heads = range(n_heads)

    def scores_into(j, s_ref):
        r0 = pl.multiple_of(j * tq, tq)
        for h in heads:
            q = q_ref[0, :, h * LANES:(h + 1) * LANES]
            s_ref[h] = _dot_nt(k_ref[0, pl.ds(r0, tq), h * LANES:(h + 1) * LANES], q)

    def values(j, p):
        return [_dot(vt_ref[j, h * V_HEAD:(h + 1) * V_HEAD, :], p(h)) for h in heads]

    def softmax(s_ref, masked):
        p_all = []
        for h in heads:
            s = s_ref[h]
            if masked:
                s = jnp.where(causal, s, MASK_VALUE)
            m_old = stat_ref[h, 0:1, :]
            m_new = jnp.maximum(m_old, jnp.max(s, axis=0, keepdims=True))
            p = jnp.exp2(s - m_new)
            alpha = jnp.exp2(m_old - m_new)
            stat_ref[h, 0:1, :] = m_new
            stat_ref[h, 1:2, :] = alpha * stat_ref[h, 1:2, :] + jnp.sum(p, axis=0, keepdims=True)
            stat_ref[h, 2:3, :] = alpha
            p_all.append(p.astype(BF16))
        return p_all

    def step(j, s_cur, p_cur, s_next, p_prev):
        pv = values(jnp.maximum(j - 1, 0), lambda h: p_prev[h])
        alpha_prev = [stat_ref[h, 2:3, :] for h in heads]
        scores_into(j + 1, s_next)
        p = softmax(s_cur, masked=False)
        for h in heads:
            p_cur[h] = p[h]
            acc_ref[h] = alpha_prev[h] * acc_ref[h] + pv[h]

    def finish(s_cur, p_prev):
        pv = values(jnp.maximum(qi - 1, 0), lambda h: p_prev[h])
        alpha_prev = [stat_ref[h, 2:3, :] for h in heads]
        p = softmax(s_cur, masked=True)
        pv_last = values(qi, lambda h: p[h])
        out = []
        for h in heads:
            acc = stat_ref[h, 2:3, :] * (alpha_prev[h] * acc_ref[h] + pv[h]) + pv_last[h]
            out.append(acc / stat_ref[h, 1:2, :])
        for pair in range(n_heads // 2):
            o_t = jnp.concatenate(out[2 * pair:2 * pair + 2], axis=0)
            o_ref[0, :, pair * LANES:(pair + 1) * LANES] = o_t.T.astype(BF16)

    row3 = lax.broadcasted_iota(jnp.int32, stat_ref.shape, 1)
    stat_ref[...] = jnp.where(row3 == 0, MASK_VALUE, jnp.where(row3 == 2, 1.0, 0.0))
    acc_ref[...] = jnp.zeros(acc_ref.shape, F32)
    p_b[...] = jnp.zeros(p_b.shape, BF16)
    scores_into(0, s_a)

    def pair_of_steps(jj, _):
        step(2 * jj, s_a, p_a, s_b, p_b)
        step(2 * jj + 1, s_b, p_b, s_a, p_a)
        return 0

    lax.fori_loop(0, qi // 2, pair_of_steps, 0)

    @pl.when(qi % 2 == 1)
    def _():
        step(qi - 1, s_a, p_a, s_b, p_b)
        finish(s_b, p_a)

    @pl.when(qi % 2 == 0)
    def _():
        finish(s_a, p_b)


def _mla_attention(q, k, vt, *, batch, seq, tq, n_heads):
    hq = MLA_HEADS * LANES
    q3 = q.reshape(batch, seq, hq)
    k3 = k.reshape(batch, seq, hq)
    n_kv = seq // tq
    out = pl.pallas_call(
        functools.partial(_attn_kernel, tq=tq, n_heads=n_heads),
        grid=(batch, MLA_HEADS // n_heads, seq // tq),
        in_specs=[
            pl.BlockSpec((1, tq, n_heads * LANES), lambda b, h, i: (b, i, h)),
            pl.BlockSpec((1, seq, n_heads * LANES), lambda b, h, i: (b, 0, h)),
            pl.BlockSpec((n_kv, n_heads * V_HEAD, tq), lambda b, h, i: (b, h, 0)),
        ],
        out_specs=pl.BlockSpec((1, tq, n_heads * V_HEAD), lambda b, h, i: (b, i, h)),
        out_shape=jax.ShapeDtypeStruct((batch, seq, MLA_HEADS * V_HEAD), BF16),
        scratch_shapes=[
            pltpu.VMEM((n_heads, tq, tq), F32), pltpu.VMEM((n_heads, tq, tq), F32),
            pltpu.VMEM((n_heads, tq, tq), BF16), pltpu.VMEM((n_heads, tq, tq), BF16),
            pltpu.VMEM((n_heads, V_HEAD, tq), F32), pltpu.VMEM((n_heads, 8, tq), F32),
        ],
        compiler_params=_params("parallel", "parallel", "parallel"),
        name="mla_attn",
    )(q3, k3, vt)
    return out.reshape(batch * seq, MLA_HEADS * V_HEAD)


def _swiglu(h, wg_ref, wu_ref, wd_ref, n_chunks):
    f = wd_ref.shape[0]
    tf = f // n_chunks
    acc = None
    for j in range(n_chunks):
        sl = slice(j * tf, (j + 1) * tf)
        g = _dot(h, wg_ref[:, sl])
        u = _dot(h, wu_ref[:, sl])
        part = _dot((_silu(g) * u).astype(BF16), wd_ref[sl, :])
        acc = part if acc is None else acc + part
    return acc


def _ffn_gdn_kernel(a_ref, wo_ref, gate1_ref, x_ref, gain_f, sc_f, sh_f, gate2_ref, wg_ref, wu_ref, wd_ref,
                    gain_g, sc_g, sh_g, w_ref, conv_ref, alog_ref, dtb_ref,
                    x_out, q_out, k_out, v_out, z_out, bg_out, xprev_ref, tail_ref, *, tiles_per_seq, n_chunks):
    i = pl.program_id(0)

    @pl.when(i == 0)
    def _():
        xprev_ref[...] = jnp.zeros(xprev_ref.shape, F32)

    @pl.when((i - 1) % tiles_per_seq == 0)
    def _():
        tail_ref[...] = jnp.zeros(tail_ref.shape, F32)

    _gdn_proj_tile(xprev_ref[...], gain_g, sc_g, sh_g, w_ref, conv_ref, alog_ref, dtb_ref,
                   q_out, k_out, v_out, z_out, bg_out, tail_ref)

    x = x_ref[...] + gate1_ref[...] * _dot(a_ref[...], wo_ref[...])
    h = _mod_norm(x, gain_f[...], sc_f[...], sh_f[...]).astype(BF16)
    x = x + gate2_ref[...] * _swiglu(h, wg_ref, wu_ref, wd_ref, n_chunks)
    x_out[...] = x
    xprev_ref[...] = x


def _gdn_proj_tile(x, gain_ref, sc_ref, sh_ref, w_ref, conv_ref, alog_ref, dtb_ref,
                   q_out, k_out, v_out, z_out, bg_out, tail_ref):
    tm = x.shape[0]
    nqk = GDN_HEADS * GDN_DK
    nqkv = 2 * nqk + GDN_HEADS * GDN_DV
    nz = GDN_HEADS * GDN_DV
    h = _mod_norm(x, gain_ref[...], sc_ref[...], sh_ref[...]).astype(BF16)

    def l2n(t):
        return t * lax.rsqrt(jnp.sum(t * t, axis=-1, keepdims=True) + EPS)

    for c0 in range(0, nqkv, GDN_PROJ_COLS):
        cols = slice(c0, c0 + GDN_PROJ_COLS)
        cur = _dot(h, w_ref[:, cols])
        ext = jnp.concatenate([tail_ref[:, cols], cur], axis=0)
        tail_ref[:, cols] = cur[tm - 8:, :]
        y = conv_ref[CONV_K - 1:CONV_K, cols] * cur
        for back in range(1, CONV_K):
            tap = CONV_K - 1 - back
            y = y + conv_ref[tap:tap + 1, cols] * pltpu.roll(ext, back, 0)[8:, :]
        act = _silu(y)
        for h0 in range(0, GDN_PROJ_COLS, GDN_DK):
            head = act[:, h0:h0 + GDN_DK]
            dst = slice(c0 % nqk + h0, c0 % nqk + h0 + GDN_DK)
            if c0 < nqk:
                q_out[:, dst] = (l2n(head) * (GDN_DK ** -0.5)).astype(BF16)
            elif c0 < 2 * nqk:
                k_out[:, dst] = l2n(head).astype(BF16)
            else:
                v_out[:, dst] = head.astype(BF16)
    for c0 in range(0, nz, GDN_PROJ_COLS):
        z_out[:, c0:c0 + GDN_PROJ_COLS] = _dot(h, w_ref[:, nqkv + c0:nqkv + c0 + GDN_PROJ_COLS]).astype(BF16)

    ba = _dot(h, w_ref[:, nqkv + nz:])
    lane = lax.broadcasted_iota(jnp.int32, ba.shape, 1) & (LANES - 1)
    beta = jax.nn.sigmoid(ba)
    pre = ba + dtb_ref[...]
    softplus = jnp.maximum(pre, 0.0) + jnp.log(1.0 + jnp.exp(-jnp.abs(pre)))
    g = jnp.where((lane >= GDN_GROUP) & (lane < 2 * GDN_GROUP), -jnp.exp(alog_ref[...]) * softplus, 0.0)
    row = lax.broadcasted_iota(jnp.int32, (tm, tm), 0)
    col = lax.broadcasted_iota(jnp.int32, (tm, tm), 1)
    same_chunk = (row & -CHUNK) == (col & -CHUNK)
    tril = jnp.where(same_chunk & (col <= row), 1.0, 0.0).astype(BF16)
    g_hi, g_mid, g_lo = _split3(g)
    gc = _dot(tril, g_hi) + (_dot(tril, g_mid) + _dot(tril, g_lo))
    bg_out[...] = jnp.where(lane < GDN_GROUP, beta, gc)


def _ffn_gdn(a, w_out, x2, mod3, gain_f, w_gate_up, w_down, gain_g, w_in, conv_w, a_log, dt_bias,
             *, batch, seq, tm):
    t, d = x2.shape
    f = w_down.shape[0]
    kdim = a.shape[1]
    nqk = GDN_HEADS * GDN_DK
    nqkv = 2 * nqk + GDN_HEADS * GDN_DV
    nz = GDN_HEADS * GDN_DV
    n_groups = GDN_HEADS // GDN_GROUP
    tiles_per_seq = seq // tm
    n_tiles = t // tm
    wb = w_in[:, nqkv + nz:nqkv + nz + GDN_HEADS]
    wa = w_in[:, nqkv + nz + GDN_HEADS:]

    def group_rows(b_part, a_part, dtype):
        rows = b_part.shape[0]
        chunks = []
        for gidx in range(n_groups):
            sl = slice(gidx * GDN_GROUP, (gidx + 1) * GDN_GROUP)
            chunks += [b_part[:, sl], a_part[:, sl], jnp.zeros((rows, LANES - 2 * GDN_GROUP), dtype)]
        return jnp.concatenate(chunks, axis=1)

    w_p = jnp.concatenate([w_in[:, :nqkv + nz], group_rows(wb, wa, F32)], axis=1).astype(BF16)
    zero_h = jnp.zeros((1, GDN_HEADS), F32)
    alog_p = group_rows(zero_h, a_log.reshape(1, GDN_HEADS), F32)
    dtb_p = group_rows(zero_h, dt_bias.reshape(1, GDN_HEADS), F32)
    wgu = w_gate_up.astype(BF16)

    ffn_tile = lambda i: jnp.minimum(i, n_tiles - 1)
    gdn_tile = lambda i: jnp.maximum(i - 1, 0)
    ffn_rows = lambda n: pl.BlockSpec((tm, n), lambda i: (ffn_tile(i), 0))
    gdn_rows = lambda n: pl.BlockSpec((tm, n), lambda i: (gdn_tile(i), 0))
    ffn_mod = lambda which: _mod_spec(0, which, batch, tiles_per_seq, d, ffn_tile)
    gdn_mod = lambda which: _mod_spec(1, which, batch, tiles_per_seq, d, gdn_tile)
    outs = pl.pallas_call(
        functools.partial(_ffn_gdn_kernel, tiles_per_seq=tiles_per_seq, n_chunks=2),
        grid=(n_tiles + 1,),
        in_specs=[
            ffn_rows(kdim), _resident((kdim, d)), ffn_mod(2),
            ffn_rows(d), _resident((1, d)), ffn_mod(4), ffn_mod(3), ffn_mod(5),
            _resident((d, f)), _resident((d, f), (0, 1)), _resident((f, d)),
            _resident((1, d)), gdn_mod(1), gdn_mod(0),
            _resident(w_p.shape), _resident((CONV_K, nqkv)),
            _resident((1, n_groups * LANES)), _resident((1, n_groups * LANES)),
        ],
        out_specs=[ffn_rows(d), gdn_rows(nqk), gdn_rows(nqk), gdn_rows(nz), gdn_rows(nz),
                   gdn_rows(n_groups * LANES)],
        out_shape=[
            jax.ShapeDtypeStruct((t, d), F32),
            jax.ShapeDtypeStruct((t, nqk), BF16),
            jax.ShapeDtypeStruct((t, nqk), BF16),
            jax.ShapeDtypeStruct((t, nz), BF16),
            jax.ShapeDtypeStruct((t, nz), BF16),
            jax.ShapeDtypeStruct((t, n_groups * LANES), F32),
        ],
        scratch_shapes=[pltpu.VMEM((tm, d), F32), pltpu.VMEM((8, nqkv), F32)],
        compiler_params=_params("arbitrary"),
        name="ffn_gdn_proj",
    )(a, w_out.astype(BF16), mod3, x2, gain_f.reshape(1, d), mod3, mod3, mod3, wgu, wgu,
      w_down.astype(BF16), gain_g.reshape(1, d), mod3, mod3, w_p, conv_w, alog_p, dtb_p)
    return outs


def _gdn_core_kernel(q_ref, k_ref, v_ref, z_ref, bg_ref, gn_ref, o_ref, state_ref):
    ts = q_ref.shape[1]
    c = CHUNK
    nc = ts // c
    nh = GDN_HEADS
    row = lax.broadcasted_iota(jnp.int32, (1, c, c), 1)
    col = lax.broadcasted_iota(jnp.int32, (1, c, c), 2)
    causal = col <= row
    strict = col < row
    eye = jnp.where(row == col, 1.0, 0.0).astype(F32)

    def level_mask(s):
        return ((row & -2 * s) == (col & -2 * s)) & ((row & s) != 0) & ((col & s) == 0)

    lane = lax.broadcasted_iota(jnp.int32, (nh * nc, c, LANES), 2)

    @pl.when(pl.program_id(1) == 0)
    def _():
        state_ref[...] = jnp.zeros(state_ref.shape, F32)

    bg = bg_ref[0]

    def stack(ref):
        return jnp.concatenate(
            [ref[0, :, hd * GDN_DK:(hd + 1) * GDN_DK].reshape(nc, c, GDN_DK) for hd in range(nh)], axis=0)

    def stack_col(first_lane):
        return jnp.concatenate(
            [bg[:, first_lane + hd:first_lane + hd + 1].reshape(nc, c, 1) for hd in range(nh)], axis=0)

    q = stack(q_ref).astype(F32)
    k16 = stack(k_ref)
    k = k16.astype(F32)
    v = stack(v_ref).astype(F32)
    beta = stack_col(0)
    gc = stack_col(nh)

    g_hi, g_mid, g_lo = (t.astype(F32) for t in _split3(gc))
    terms = lambda base, other: jnp.where(lane == base, g_hi, jnp.where(
        lane == base + 1, g_mid, jnp.where(lane == base + 2, g_lo, other)))
    lhs = terms(0, jnp.where(lane < 6, 1.0, 0.0)).astype(BF16)
    rhs = terms(3, jnp.where(lane < 3, -1.0, 0.0)).astype(BF16)
    diff = -_bmm_nt(lhs, rhs)
    decay = jnp.where(causal, jnp.exp(jnp.where(causal, diff, 0.0)), 0.0)

    kb = k * beta
    a_mat = jnp.where(strict, _bmm_nt(kb.astype(BF16), k16) * decay, 0.0)
    attn = jnp.where(causal, _bmm_nt(q.astype(BF16), k16) * decay, 0.0).astype(BF16)

    inv = eye - jnp.where(level_mask(1), a_mat, 0.0)
    s = 2
    while s < c:
        x_s = jnp.where(level_mask(s), a_mat, 0.0).astype(BF16)
        inv16 = inv.astype(BF16)
        inv = inv - _bmm(_bmm(inv16, x_s).astype(BF16), inv16)
        s *= 2

    eg = jnp.exp(gc)
    g_last = gc[:, c - 1:c, :]
    rhs_uw = jnp.concatenate([v * beta, kb * eg], axis=-1).astype(BF16)
    uw = _bmm(inv.astype(BF16), rhs_uw)
    a_uw = _bmm(attn, uw.astype(BF16))
    o_intra = a_uw[..., :GDN_DV]
    q_eff = (q * eg - a_uw[..., GDN_DV:]).astype(BF16)
    kt_uw = _bmm_tn(k16, (uw * jnp.exp(g_last - gc)).astype(BF16))
    s_add = kt_uw[..., :GDN_DV]
    s_mul = kt_uw[..., GDN_DV:].astype(BF16)
    dec = jnp.exp(g_last)

    per_head = lambda a: a.reshape((nh, nc) + a.shape[1:])
    q_eff, o_intra, s_add, s_mul, dec = map(per_head, (q_eff, o_intra, s_add, s_mul, dec))
    st = state_ref[...]
    outs = []
    for n in range(nc):
        st16 = st.astype(BF16)
        outs.append(_bmm(q_eff[:, n], st16) + o_intra[:, n])
        st = dec[:, n] * st - _bmm(s_mul[:, n], st16) + s_add[:, n]
    state_ref[...] = st
    o = jnp.stack(outs, axis=1).reshape(nh, ts, GDN_DV)
    gn = gn_ref[...]
    for hd in range(nh):
        sl = slice(hd * GDN_DV, (hd + 1) * GDN_DV)
        z = z_ref[0, :, sl].astype(F32)
        o_ref[0, :, sl] = (_rms(o[hd]) * gn * _silu(z)).astype(BF16)


def _gdn_core(q, k, v, z, bg, out_norm, *, batch, seq, ts):
    width = GDN_HEADS * GDN_DK
    shp = lambda a: a.reshape(batch, seq, a.shape[-1])
    blk = pl.BlockSpec((1, ts, width), lambda b, i: (b, i, 0))
    out = pl.pallas_call(
        _gdn_core_kernel,
        grid=(batch, seq // ts),
        in_specs=[blk, blk, blk, blk,
                  pl.BlockSpec((1, ts, LANES), lambda b, i: (b, i, 0)),
                  pl.BlockSpec((1, GDN_DV), lambda b, i: (0, 0))],
        out_specs=blk,
        out_shape=jax.ShapeDtypeStruct((batch, seq, GDN_HEADS * GDN_DV), BF16),
        scratch_shapes=[pltpu.VMEM((GDN_HEADS, GDN_DK, GDN_DV), F32)],
        compiler_params=_params("parallel", "arbitrary"),
        name="gdn_core",
    )(shp(q), shp(k), shp(v), shp(z), shp(bg), out_norm.reshape(1, GDN_DV))
    return out.reshape(batch * seq, GDN_HEADS * GDN_DV)


def _moe_slots(tb):
    slots = MOE_TOP_K * tb + N_EXPERTS * MOE_GRANULE
    return slots, slots // MOE_GRANULE


def _moe_route_kernel(a_ref, wo_ref, gate1_ref, x_ref, gain_ref, sc_ref, sh_ref, wr_ref,
                      x_out, hs_ref, rt_ref, meta_ref):
    tb = x_ref.shape[0]
    slots = hs_ref.shape[0]
    x = x_ref[...] + gate1_ref[...] * _dot(a_ref[...], wo_ref[...])
    x_out[...] = x
    h = _mod_norm(x, gain_ref[...], sc_ref[...], sh_ref[...])
    h_hi, h_mid, _ = _split3(h)
    logits = _dot(h_hi, wr_ref[0]) + (_dot(h_hi, wr_ref[1]) + _dot(h_mid, wr_ref[0]))
    lane = lax.broadcasted_iota(jnp.int32, (tb, LANES), 1)
    lane_f = lane.astype(F32)
    logits = jnp.where(lane < N_EXPERTS, logits, -jnp.inf)
    m1 = jnp.max(logits, axis=-1, keepdims=True)
    i1 = jnp.min(jnp.where(logits == m1, lane_f, float(LANES)), axis=-1, keepdims=True)
    rest = jnp.where(lane_f == i1, -jnp.inf, logits)
    m2 = jnp.max(rest, axis=-1, keepdims=True)
    i2 = jnp.min(jnp.where(rest == m2, lane_f, float(LANES)), axis=-1, keepdims=True)
    e2 = jnp.exp(m2 - m1)
    w1 = 1.0 / (1.0 + e2)
    w2 = e2 * w1

    first = lane_f == i1
    second = lane_f == i2
    chosen = jnp.where(first, 1.0, jnp.where(second, 1.0, 0.0))
    r = lax.broadcasted_iota(jnp.int32, (tb, tb), 0)
    c = lax.broadcasted_iota(jnp.int32, (tb, tb), 1)
    earlier = jnp.where(c < r, 1.0, 0.0).astype(BF16)
    rank = _dot(earlier, chosen.astype(BF16))
    count = jnp.sum(chosen, axis=0, keepdims=True)
    granules = jnp.floor((count + (MOE_GRANULE - 1)) * (1.0 / MOE_GRANULE))
    er = lax.broadcasted_iota(jnp.int32, (LANES, LANES), 0)
    ec = lax.broadcasted_iota(jnp.int32, (LANES, LANES), 1)
    before = jnp.where(er < ec, 1.0, 0.0).astype(BF16)
    seg_gran = _dot(jnp.broadcast_to(granules, (8, LANES)).astype(BF16), before)[0:1]
    slot_of = seg_gran * MOE_GRANULE + rank
    pos1 = jnp.sum(jnp.where(first, slot_of, 0.0), axis=-1, keepdims=True)
    pos2 = jnp.sum(jnp.where(second, slot_of, 0.0), axis=-1, keepdims=True)

    slot = lax.broadcasted_iota(jnp.int32, (tb, slots), 1).astype(F32)
    place = jnp.where(slot == pos1, 1.0, jnp.where(slot == pos2, 1.0, 0.0)).astype(BF16)
    hs_ref[...] = _dot_tn(place, h_hi).astype(BF16)
    rt_ref[...] = jnp.where(lane == 0, pos1, jnp.where(lane == 1, pos2, jnp.where(
        lane == 2, w1, jnp.where(lane == 3, w2, 0.0))))
    row8 = lax.broadcasted_iota(jnp.int32, (8, LANES), 0)
    meta_ref[0] = jnp.where(row8 == 0, granules, jnp.where(row8 == 1, seg_gran, 0.0))


def _moe_expert_kernel(src_ref, te_ref, used_ref, *refs):
    n_gran = len(refs) - 4
    wg_ref, wu_ref, wd_ref, o_ref = refs[n_gran:]
    step = pl.program_id(0)

    @pl.when(step < used_ref[0])
    def _():
        h = jnp.concatenate([g[...] for g in refs[:n_gran]], axis=0)
        o_ref[...] = _swiglu(h, wg_ref.at[0], wu_ref.at[0], wd_ref.at[0], 1).astype(BF16)

    @pl.when(step >= used_ref[0])
    def _():
        o_ref[...] = jnp.zeros(o_ref.shape, BF16)


def _moe_combine_kernel(inv_ref, *refs):
    n_gran = len(refs) - 5
    rt_ref, x_ref, gate_ref, fn_ref, o_ref = refs[n_gran:]
    tb = x_ref.shape[0]
    ys = jnp.concatenate([g[...] for g in refs[:n_gran]], axis=0)
    rt = rt_ref[...]
    pos1, pos2, w1, w2 = rt[:, 0:1], rt[:, 1:2], rt[:, 2:3], rt[:, 3:4]
    slot = lax.broadcasted_iota(jnp.int32, (tb, ys.shape[0]), 1).astype(F32)
    y1 = _dot(jnp.where(slot == pos1, 1.0, 0.0).astype(BF16), ys)
    y2 = _dot(jnp.where(slot == pos2, 1.0, 0.0).astype(BF16), ys)
    xo = x_ref[...] + gate_ref[...] * (w1 * y1 + w2 * y2)
    o_ref[...] = _rms(xo) * fn_ref[...]


def _moe_tables(meta, n_steps, gran_per_step, gran_per_tile):
    i32 = jnp.int32
    cnt = meta[:, 0, :N_EXPERTS].astype(i32)
    seg = meta[:, 1, :N_EXPERTS].astype(i32)
    n_tiles = cnt.shape[0]
    earlier = jnp.cumsum(cnt, axis=0) - cnt
    steps = (jnp.sum(cnt, axis=0) + gran_per_step - 1) // gran_per_step
    step_end = jnp.cumsum(steps)
    used = step_end[-1:]
    seg_pos = (step_end - steps)[None, :] * gran_per_step + earlier
    seg_src = jnp.arange(n_tiles, dtype=i32)[:, None] * gran_per_tile + seg

    s = jnp.arange(n_steps * gran_per_step, dtype=i32)[:, None]
    pos, length, start = (a.reshape(1, -1) for a in (seg_pos, cnt, seg_src))
    src = jnp.sum(jnp.where((s >= pos) & (s < pos + length), start - pos + s, 0), axis=1)
    st = jnp.arange(n_steps, dtype=i32)[:, None]
    step_expert = jnp.minimum(jnp.sum((step_end[None, :] <= st).astype(i32), axis=1), N_EXPERTS - 1)

    g = jnp.arange(gran_per_tile, dtype=i32)[None, :, None]
    lo, n, base = (a[:, None, :] for a in (seg, cnt, seg_pos))
    inv = jnp.sum(jnp.where((g >= lo) & (g < lo + n), base + g - lo, 0), axis=-1).reshape(-1)
    return src, step_expert, used, inv


def _mixer_moe(a, w_out, x2, mod3, gain, w_router, w_gate_up, w_down, final_norm, *, layer, batch, seq, tb):
    t, d = x2.shape
    kdim = a.shape[1]
    n_e, fe, _ = w_down.shape
    tiles_per_seq = seq // tb
    n_tiles = t // tb
    slots, gran_per_tile = _moe_slots(tb)
    gran_per_step = MOE_LHS_ROWS // MOE_GRANULE
    n_steps = pl.cdiv(MOE_TOP_K * t // MOE_GRANULE + n_tiles * n_e, gran_per_step) + n_e
    wr = jnp.pad(w_router, ((0, 0), (0, LANES - n_e)))
    wr_hi = wr.astype(BF16)
    wr_lo = (wr - wr_hi.astype(F32)).astype(BF16)
    wr2 = jnp.stack([wr_hi, wr_lo])
    wgu = w_gate_up.astype(BF16)
    row_spec = lambda rows, cols: pl.BlockSpec((rows, cols), lambda i, *_: (i, 0))

    x2, hs, rt, meta = pl.pallas_call(
        _moe_route_kernel,
        grid=(n_tiles,),
        in_specs=[
            row_spec(tb, kdim), _resident((kdim, d)),
            _mod_spec(layer, 2, batch, tiles_per_seq, d),
            row_spec(tb, d), _resident((1, d)),
            _mod_spec(layer, 4, batch, tiles_per_seq, d),
            _mod_spec(layer, 3, batch, tiles_per_seq, d),
            _resident((2, d, LANES)),
        ],
        out_specs=[row_spec(tb, d), row_spec(slots, d), row_spec(tb, LANES),
                   pl.BlockSpec((1, 8, LANES), lambda i: (i, 0, 0))],
        out_shape=[
            jax.ShapeDtypeStruct((t, d), F32),
            jax.ShapeDtypeStruct((n_tiles * slots, d), BF16),
            jax.ShapeDtypeStruct((t, LANES), F32),
            jax.ShapeDtypeStruct((n_tiles, 8, LANES), F32),
        ],
        compiler_params=_params("parallel"),
        name="moe_route",
    )(a, w_out.astype(BF16), mod3, x2, gain.reshape(1, d), mod3, mod3, wr2)

    src, step_expert, used, inv = _moe_tables(meta, n_steps, gran_per_step, gran_per_tile)

    def granule_spec(j, per_step):
        return pl.BlockSpec((MOE_GRANULE, d), lambda i, tbl, *_: (tbl[i * per_step + j], 0))

    ye = pl.pallas_call(
        _moe_expert_kernel,
        grid_spec=pltpu.PrefetchScalarGridSpec(
            num_scalar_prefetch=3,
            grid=(n_steps,),
            in_specs=[granule_spec(j, gran_per_step) for j in range(gran_per_step)] + [
                pl.BlockSpec((1, d, fe), lambda i, src, te, used: (te[i], 0, 0)),
                pl.BlockSpec((1, d, fe), lambda i, src, te, used: (te[i], 0, 1)),
                pl.BlockSpec((1, fe, d), lambda i, src, te, used: (te[i], 0, 0)),
            ],
            out_specs=pl.BlockSpec((MOE_LHS_ROWS, d), lambda i, *_: (i, 0)),
        ),
        out_shape=jax.ShapeDtypeStruct((n_steps * MOE_LHS_ROWS, d), BF16),
        compiler_params=_params("arbitrary"),
        name="moe_experts",
    )(src, step_expert, used, *([hs] * gran_per_step), wgu, wgu, w_down.astype(BF16))

    return pl.pallas_call(
        _moe_combine_kernel,
        grid_spec=pltpu.PrefetchScalarGridSpec(
            num_scalar_prefetch=1,
            grid=(n_tiles,),
            in_specs=[granule_spec(j, gran_per_tile) for j in range(gran_per_tile)] + [
                row_spec(tb, LANES), row_spec(tb, d),
                _mod_spec(layer, 5, batch, tiles_per_seq, d),
                pl.BlockSpec((1, d), lambda i, *_: (0, 0)),
            ],
            out_specs=row_spec(tb, d),
        ),
        out_shape=jax.ShapeDtypeStruct((t, d), F32),
        compiler_params=_params("parallel"),
        name="moe_combine",
    )(inv, *([ye] * gran_per_tile), rt, x2, mod3, final_norm.reshape(1, d))


def kernel(x, c, positions, ada_w, ada_b, norm_mix, norm_ffn, mla_w_in, mla_q_norm, mla_w_qb, mla_kv_norm, mla_w_kvb, mla_w_out, ffn_w_gate_up, ffn_w_down, gdn_w_in, gdn_conv_w, gdn_a_log, gdn_dt_bias, gdn_out_norm, gdn_w_out, moe_w_router, moe_w_gate_up, moe_w_down, final_norm):
    batch, seq, d = x.shape
    depth = ada_w.shape[0]
    assert depth == 2 and seq % 512 == 0
    t = batch * seq
    tm = 512
    dims = dict(batch=batch, seq=seq, tm=tm)

    mod = _adaln(c, ada_w, ada_b)
    mod3 = mod.reshape(depth * batch * N_MOD, 1, d)
    x2 = x.reshape(t, d)
    pos = positions.astype(F32).reshape(t, 1)

    tq = 256
    q, k, vt = _mla_proj(x2, pos, mod3, norm_mix[0], mla_w_in[0], mla_q_norm[0], mla_w_qb[0],
                         mla_kv_norm[0], mla_w_kvb[0], layer=0, tkv=tq, **dims)
    attn = _mla_attention(q, k, vt, batch=batch, seq=seq, tq=tq, n_heads=8)
    x2, gq, gk, gv, gz, bg = _ffn_gdn(attn, mla_w_out[0], x2, mod3, norm_ffn[0], ffn_w_gate_up[0],
                                      ffn_w_down[0], norm_mix[1], gdn_w_in[0], gdn_conv_w[0],
                                      gdn_a_log[0], gdn_dt_bias[0], **dims)
    og = _gdn_core(gq, gk, gv, gz, bg, gdn_out_norm[0], batch=batch, seq=seq, ts=256)
    out = _mixer_moe(og, gdn_w_out[0], x2, mod3, norm_ffn[1], moe_w_router[0], moe_w_gate_up[0],
                     moe_w_down[0], final_norm, layer=1, batch=batch, seq=seq, tb=256)
    return out.reshape(batch, seq, d)
```

```python
import functools
import math

import jax
import jax.numpy as jnp
from jax import lax
from jax.experimental import pallas as pl
from jax.experimental.pallas import tpu as pltpu

F32 = jnp.float32
BF16 = jnp.bfloat16

EPS = 1e-6
N_MOD = 6
LANES = 128
VMEM_LIMIT_BYTES = 60 * 1024 * 1024

MLA_HEADS = 16
Q_LORA = 512
KV_LORA = 256
QK_NOPE = 64
QK_ROPE = 32
V_HEAD = 64
ROPE_THETA = 10000.0
MASK_VALUE = -1e30
ATTN_SUM_ROWS = 16
GDN_HEADS = 8
GDN_DK = 128
GDN_DV = 128
CONV_K = 4
CHUNK = 64
GDN_GROUP = GDN_HEADS
GDN_PROJ_COLS = 256
N_EXPERTS = 8
MOE_TOP_K = 2
MOE_GRANULE = 16
MOE_LHS_ROWS = 512
MOE_ROUTE_TILES = 2


def _params(*sem):
    return pltpu.CompilerParams(dimension_semantics=sem, vmem_limit_bytes=VMEM_LIMIT_BYTES)


def _resident(shape, index=None):
    index = (0,) * len(shape) if index is None else index
    return pl.BlockSpec(shape, lambda *_: index, pipeline_mode=pl.Buffered(1))


def _split3(a):
    hi = a.astype(BF16)
    r = a - hi.astype(F32)
    mid = r.astype(BF16)
    lo = (r - mid.astype(F32)).astype(BF16)
    return hi, mid, lo


def _dot(a, b):
    return jnp.dot(a, b, preferred_element_type=F32)


def _dot_nt(a, b):
    return lax.dot_general(a, b, (((1,), (1,)), ((), ())), preferred_element_type=F32)


def _dot_tn(a, b):
    return lax.dot_general(a, b, (((0,), (0,)), ((), ())), preferred_element_type=F32)


def _bmm(a, b):
    return lax.dot_general(a, b, (((2,), (1,)), ((0,), (0,))), preferred_element_type=F32)


def _bmm_nt(a, b):
    return lax.dot_general(a, b, (((2,), (2,)), ((0,), (0,))), preferred_element_type=F32)


def _bmm_tn(a, b):
    return lax.dot_general(a, b, (((1,), (1,)), ((0,), (0,))), preferred_element_type=F32)


def _dot_f32(a, b):
    a_hi, a_mid, _ = _split3(a)
    b_hi, b_mid, _ = _split3(b)
    return _dot(a_hi, b_hi) + (_dot(a_hi, b_mid) + _dot(a_mid, b_hi))


def _rms(x):
    return x * lax.rsqrt(jnp.mean(x * x, axis=-1, keepdims=True) + EPS)


def _silu(x):
    return x * jax.nn.sigmoid(x)


def _mod_norm(x, gain, sc, sh):
    return _rms(x) * gain * (1.0 + sc) + sh


def _adaln_kernel(c_ref, w_ref, b_ref, o_ref):
    c = c_ref[...]
    cond = _silu(c)
    o_ref[0] = _dot_f32(cond, w_ref[0]) + b_ref[0]


def _adaln(c, ada_w, ada_b):
    depth, d, n = ada_w.shape
    b = c.shape[0]
    tn = d
    return pl.pallas_call(
        _adaln_kernel,
        grid=(depth, n // tn),
        in_specs=[
            pl.BlockSpec((b, d), lambda l, j: (0, 0)),
            pl.BlockSpec((1, d, tn), lambda l, j: (l, 0, j)),
            pl.BlockSpec((1, 1, tn), lambda l, j: (l, 0, j)),
        ],
        out_specs=pl.BlockSpec((1, b, tn), lambda l, j: (l, 0, j)),
        out_shape=jax.ShapeDtypeStruct((depth, b, n), F32),
        compiler_params=_params("parallel", "parallel"),
        name="adaln",
    )(c, ada_w, ada_b.reshape(depth, 1, n))


def _mod_spec(layer, which, batch, tiles_per_seq, d, tile_of=lambda i: i):
    def index(i, *_):
        return ((layer * batch + tile_of(i) // tiles_per_seq) * N_MOD + which, 0, 0)
    return pl.BlockSpec((None, 1, d), index)


def _mla_proj_kernel(x_ref, pos_ref, gain_ref, sc_ref, sh_ref, win_ref, qn_ref, wqb_ref,
                     kvn_ref, wk_ref, wv_ref, freq_ref, sign_ref, q_out, k_out, vt_out, *, scale):
    tm = x_ref.shape[0]
    h = _mod_norm(x_ref[...], gain_ref[...], sc_ref[...], sh_ref[...]).astype(BF16)
    proj = _dot(h, win_ref[...])
    q_lat = proj[:, :Q_LORA]
    kv_lat = proj[:, Q_LORA:Q_LORA + KV_LORA]
    k_rope = proj[:, Q_LORA + KV_LORA:]
    qn = (_rms(q_lat) * qn_ref[...]).astype(BF16)
    kvn = (_rms(kv_lat) * kvn_ref[...]).astype(BF16)

    ang = pos_ref[...] * freq_ref[...]
    cos = jnp.cos(ang)
    sin = jnp.sin(ang) * sign_ref[...]
    lane = lax.broadcasted_iota(jnp.int32, (tm, LANES), 1)
    first_half = lane < QK_NOPE + QK_ROPE // 2

    def rope(t, cos, sin):
        partner = jnp.where(first_half, pltpu.roll(t, LANES - QK_ROPE // 2, 1),
                            pltpu.roll(t, QK_ROPE // 2, 1))
        return t * cos + partner * sin

    q = _dot(qn, wqb_ref[...])
    k = _dot(kvn, wk_ref[...])
    k_rope = rope(k_rope, cos, sin)
    cos_q = cos * scale
    sin_q = sin * scale
    for hd in range(MLA_HEADS):
        sl = slice(hd * LANES, (hd + 1) * LANES)
        q_out[:, sl] = rope(q[:, sl], cos_q, sin_q).astype(BF16)
        k_out[:, sl] = (k[:, sl] + k_rope).astype(BF16)
    vt = _dot(kvn, wv_ref[...]).T
    tkv = vt_out.shape[2]
    for j in range(vt_out.shape[0]):
        vt_out[j] = vt[:, j * tkv:(j + 1) * tkv].astype(BF16)


def _mla_proj(x2, pos, mod3, gain, w_in, q_norm, w_qb, kv_norm, w_kvb, *, layer, batch, seq, tm, tkv):
    t, d = x2.shape
    tiles_per_seq = seq // tm
    hq = MLA_HEADS * LANES
    pad_in = jnp.zeros((d, LANES), F32).at[:, QK_NOPE:QK_NOPE + QK_ROPE].set(w_in[:, Q_LORA + KV_LORA:])
    w_in_p = jnp.concatenate([w_in[:, :Q_LORA + KV_LORA], pad_in], axis=1).astype(BF16)
    w_qb_p = jnp.pad(w_qb.reshape(Q_LORA, MLA_HEADS, QK_NOPE + QK_ROPE),
                     ((0, 0), (0, 0), (0, LANES - QK_NOPE - QK_ROPE))).reshape(Q_LORA, hq).astype(BF16)
    w_kv3 = w_kvb.reshape(KV_LORA, MLA_HEADS, QK_NOPE + V_HEAD)
    w_k_p = jnp.pad(w_kv3[:, :, :QK_NOPE], ((0, 0), (0, 0), (0, LANES - QK_NOPE))
                    ).reshape(KV_LORA, hq).astype(BF16)
    w_v = w_kv3[:, :, QK_NOPE:].reshape(KV_LORA, MLA_HEADS * V_HEAD).astype(BF16)
    half = QK_ROPE // 2
    inv_freq = ROPE_THETA ** (-jnp.arange(half, dtype=F32) / half)
    zeros = jnp.zeros((QK_NOPE,), F32)
    tail = jnp.zeros((LANES - QK_NOPE - QK_ROPE,), F32)
    freq = jnp.concatenate([zeros, inv_freq, inv_freq, tail]).reshape(1, LANES)
    sign = jnp.concatenate([zeros, -jnp.ones((half,), F32), jnp.ones((half,), F32), tail]).reshape(1, LANES)
    scale = float(QK_NOPE + QK_ROPE) ** -0.5 * math.log2(math.e)

    const = _resident
    return pl.pallas_call(
        functools.partial(_mla_proj_kernel, scale=scale),
        grid=(t // tm,),
        in_specs=[
            pl.BlockSpec((tm, d), lambda i: (i, 0)),
            pl.BlockSpec((tm, 1), lambda i: (i, 0)),
            const((1, d)),
            _mod_spec(layer, 1, batch, tiles_per_seq, d),
            _mod_spec(layer, 0, batch, tiles_per_seq, d),
            const(w_in_p.shape), const((1, Q_LORA)), const(w_qb_p.shape),
            const((1, KV_LORA)), const(w_k_p.shape), const(w_v.shape),
            const((1, LANES)), const((1, LANES)),
        ],
        out_specs=[
            pl.BlockSpec((tm, hq), lambda i: (i, 0)),
            pl.BlockSpec((tm, hq), lambda i: (i, 0)),
            pl.BlockSpec((tm // tkv, MLA_HEADS * V_HEAD, tkv), lambda i: (i, 0, 0)),
        ],
        out_shape=[
            jax.ShapeDtypeStruct((t, hq), BF16),
            jax.ShapeDtypeStruct((t, hq), BF16),
            jax.ShapeDtypeStruct((t // tkv, MLA_HEADS * V_HEAD, tkv), BF16),
        ],
        compiler_params=_params("parallel"),
        name="mla_proj",
    )(x2, pos, gain.reshape(1, d), mod3, mod3, w_in_p, q_norm.reshape(1, Q_LORA), w_qb_p,
      kv_norm.reshape(1, KV_LORA), w_k_p, w_v, freq, sign)


def _attn_kernel(q_ref, k_ref, vt_ref, o_ref, s_a, s_b, p_a, p_b, acc_ref, stat_ref, *, tq, n_heads):
    qi = pl.program_id(2)
    kv_idx = lax.broadcasted_iota(jnp.int32, (tq, tq), 0)
    q_idx = lax.broadcasted_iota(jnp.int32, (tq, tq), 1)
    causal = kv_idx <= q_idx
    heads = range(n_heads)

    def scores_into(j, s_ref):
        r0 = pl.multiple_of(j * tq, tq)
        for h in heads:
            q = q_ref[0, :, h * LANES:(h + 1) * LANES]
            s_ref[h] = _dot_nt(k_ref[0, pl.ds(r0, tq), h * LANES:(h + 1) * LANES], q)

    ones_rows = jnp.ones((ATTN_SUM_ROWS, tq), BF16)

    def values(j, p):
        return [_dot(jnp.concatenate([vt_ref[j, h * V_HEAD:(h + 1) * V_HEAD, :], ones_rows], axis=0), p(h))
                for h in heads]

    def softmax(s_ref, masked):
        p_all = []
        for h in heads:
            s = s_ref[h]
            if masked:
                s = jnp.where(causal, s, MASK_VALUE)
            m_old = stat_ref[h, 0:1, :]
            m_new = jnp.maximum(m_old, jnp.max(s, axis=0, keepdims=True))
            p = jnp.exp2(s - m_new)
            alpha = jnp.exp2(m_old - m_new)
            stat_ref[h, 0:1, :] = m_new
            stat_ref[h, 1:2, :] = alpha
            p_all.append(p.astype(BF16))
        return p_all

    def step(j, s_cur, p_cur, s_next, p_prev):
        pv = values(jnp.maximum(j - 1, 0), lambda h: p_prev[h])
        alpha_prev = [stat_ref[h, 1:2, :] for h in heads]
        scores_into(j + 1, s_next)
        p = softmax(s_cur, masked=False)
        for h in heads:
            p_cur[h] = p[h]
            acc_ref[h] = alpha_prev[h] * acc_ref[h] + pv[h]

    def finish(s_cur, p_prev):
        pv = values(jnp.maximum(qi - 1, 0), lambda h: p_prev[h])
        alpha_prev = [stat_ref[h, 1:2, :] for h in heads]
        p = softmax(s_cur, masked=True)
        pv_last = values(qi, lambda h: p[h])
        out = []
        for h in heads:
            acc = stat_ref[h, 1:2, :] * (alpha_prev[h] * acc_ref[h] + pv[h]) + pv_last[h]
            out.append(acc[:V_HEAD] / acc[V_HEAD:V_HEAD + 1])
        for pair in range(n_heads // 2):
            o_t = jnp.concatenate(out[2 * pair:2 * pair + 2], axis=0)
            o_ref[0, :, pair * LANES:(pair + 1) * LANES] = o_t.T.astype(BF16)

    row3 = lax.broadcasted_iota(jnp.int32, stat_ref.shape, 1)
    stat_ref[...] = jnp.where(row3 == 0, MASK_VALUE, 1.0)
    acc_ref[...] = jnp.zeros(acc_ref.shape, F32)
    p_b[...] = jnp.zeros(p_b.shape, BF16)
    scores_into(0, s_a)

    def pair_of_steps(jj, _):
        step(2 * jj, s_a, p_a, s_b, p_b)
        step(2 * jj + 1, s_b, p_b, s_a, p_a)
        return 0

    lax.fori_loop(0, qi // 2, pair_of_steps, 0)

    @pl.when(qi % 2 == 1)
    def _():
        step(qi - 1, s_a, p_a, s_b, p_b)
        finish(s_b, p_a)

    @pl.when(qi % 2 == 0)
    def _():
        finish(s_a, p_b)


def _mla_attention(q, k, vt, *, batch, seq, tq, n_heads):
    hq = MLA_HEADS * LANES
    q3 = q.reshape(batch, seq, hq)
    k3 = k.reshape(batch, seq, hq)
    n_kv = seq // tq
    out = pl.pallas_call(
        functools.partial(_attn_kernel, tq=tq, n_heads=n_heads),
        grid=(batch, MLA_HEADS // n_heads, seq // tq),
        in_specs=[
            pl.BlockSpec((1, tq, n_heads * LANES), lambda b, h, i: (b, i, h)),
            pl.BlockSpec((1, seq, n_heads * LANES), lambda b, h, i: (b, 0, h)),
            pl.BlockSpec((n_kv, n_heads * V_HEAD, tq), lambda b, h, i: (b, h, 0)),
        ],
        out_specs=pl.BlockSpec((1, tq, n_heads * V_HEAD), lambda b, h, i: (b, i, h)),
        out_shape=jax.ShapeDtypeStruct((batch, seq, MLA_HEADS * V_HEAD), BF16),
        scratch_shapes=[
            pltpu.VMEM((n_heads, tq, tq), F32), pltpu.VMEM((n_heads, tq, tq), F32),
            pltpu.VMEM((n_heads, tq, tq), BF16), pltpu.VMEM((n_heads, tq, tq), BF16),
            pltpu.VMEM((n_heads, V_HEAD + ATTN_SUM_ROWS, tq), F32), pltpu.VMEM((n_heads, 8, tq), F32),
        ],
        compiler_params=_params("parallel", "parallel", "parallel"),
        name="mla_attn",
    )(q3, k3, vt)
    return out.reshape(batch * seq, MLA_HEADS * V_HEAD)


def _swiglu(h, wg_ref, wu_ref, wd_ref, n_chunks):
    f = wd_ref.shape[0]
    tf = f // n_chunks
    acc = None
    for j in range(n_chunks):
        sl = slice(j * tf, (j + 1) * tf)
        g = _dot(h, wg_ref[:, sl])
        u = _dot(h, wu_ref[:, sl])
        part = _dot((_silu(g) * u).astype(BF16), wd_ref[sl, :])
        acc = part if acc is None else acc + part
    return acc


def _ffn_gdn_kernel(a_ref, wo_ref, gate1_ref, x_ref, gain_f, sc_f, sh_f, gate2_ref, wg_ref, wu_ref, wd_ref,
                    gain_g, sc_g, sh_g, w_ref, conv_ref, alog_ref, dtb_ref,
                    x_out, q_out, k_out, v_out, z_out, bg_out, xprev_ref, tail_ref, *, tiles_per_seq, n_chunks):
    i = pl.program_id(0)

    @pl.when(i == 0)
    def _():
        xprev_ref[...] = jnp.zeros(xprev_ref.shape, F32)

    @pl.when((i - 1) % tiles_per_seq == 0)
    def _():
        tail_ref[...] = jnp.zeros(tail_ref.shape, F32)

    _gdn_proj_tile(xprev_ref[...], gain_g, sc_g, sh_g, w_ref, conv_ref, alog_ref, dtb_ref,
                   q_out, k_out, v_out, z_out, bg_out, tail_ref)

    x = x_ref[...] + gate1_ref[...] * _dot(a_ref[...], wo_ref[...])
    h = _mod_norm(x, gain_f[...], sc_f[...], sh_f[...]).astype(BF16)
    x = x + gate2_ref[...] * _swiglu(h, wg_ref, wu_ref, wd_ref, n_chunks)
    x_out[...] = x
    xprev_ref[...] = x


def _gdn_proj_tile(x, gain_ref, sc_ref, sh_ref, w_ref, conv_ref, alog_ref, dtb_ref,
                   q_out, k_out, v_out, z_out, bg_out, tail_ref):
    tm = x.shape[0]
    nqk = GDN_HEADS * GDN_DK
    nqkv = 2 * nqk + GDN_HEADS * GDN_DV
    nz = GDN_HEADS * GDN_DV
    h = _mod_norm(x, gain_ref[...], sc_ref[...], sh_ref[...]).astype(BF16)

    def l2n(t):
        return t * lax.rsqrt(jnp.sum(t * t, axis=-1, keepdims=True) + EPS)

    for c0 in range(0, nqkv, GDN_PROJ_COLS):
        cols = slice(c0, c0 + GDN_PROJ_COLS)
        cur = _dot(h, w_ref[:, cols])
        ext = jnp.concatenate([tail_ref[:, cols], cur], axis=0)
        tail_ref[:, cols] = cur[tm - 8:, :]
        y = conv_ref[CONV_K - 1:CONV_K, cols] * cur
        for back in range(1, CONV_K):
            tap = CONV_K - 1 - back
            y = y + conv_ref[tap:tap + 1, cols] * pltpu.roll(ext, back, 0)[8:, :]
        act = _silu(y)
        for h0 in range(0, GDN_PROJ_COLS, GDN_DK):
            head = act[:, h0:h0 + GDN_DK]
            dst = slice(c0 % nqk + h0, c0 % nqk + h0 + GDN_DK)
            if c0 < nqk:
                q_out[:, dst] = (l2n(head) * (GDN_DK ** -0.5)).astype(BF16)
            elif c0 < 2 * nqk:
                k_out[:, dst] = l2n(head).astype(BF16)
            else:
                v_out[:, dst] = head.astype(BF16)
    for c0 in range(0, nz, GDN_PROJ_COLS):
        z_out[:, c0:c0 + GDN_PROJ_COLS] = _dot(h, w_ref[:, nqkv + c0:nqkv + c0 + GDN_PROJ_COLS]).astype(BF16)

    ba = _dot(h, w_ref[:, nqkv + nz:])
    lane = lax.broadcasted_iota(jnp.int32, ba.shape, 1) & (LANES - 1)
    beta = jax.nn.sigmoid(ba)
    pre = ba + dtb_ref[...]
    softplus = jnp.maximum(pre, 0.0) + jnp.log(1.0 + jnp.exp(-jnp.abs(pre)))
    g = jnp.where((lane >= GDN_GROUP) & (lane < 2 * GDN_GROUP), -jnp.exp(alog_ref[...]) * softplus, 0.0)
    row = lax.broadcasted_iota(jnp.int32, (tm, tm), 0)
    col = lax.broadcasted_iota(jnp.int32, (tm, tm), 1)
    same_chunk = (row & -CHUNK) == (col & -CHUNK)
    tril = jnp.where(same_chunk & (col <= row), 1.0, 0.0).astype(BF16)
    g_hi, g_mid, g_lo = _split3(g)
    gc = _dot(tril, g_hi) + (_dot(tril, g_mid) + _dot(tril, g_lo))
    bg_out[...] = jnp.where(lane < GDN_GROUP, beta, gc)


def _ffn_gdn(a, w_out, x2, mod3, gain_f, w_gate_up, w_down, gain_g, w_in, conv_w, a_log, dt_bias,
             *, batch, seq, tm):
    t, d = x2.shape
    f = w_down.shape[0]
    kdim = a.shape[1]
    nqk = GDN_HEADS * GDN_DK
    nqkv = 2 * nqk + GDN_HEADS * GDN_DV
    nz = GDN_HEADS * GDN_DV
    n_groups = GDN_HEADS // GDN_GROUP
    tiles_per_seq = seq // tm
    n_tiles = t // tm
    wb = w_in[:, nqkv + nz:nqkv + nz + GDN_HEADS]
    wa = w_in[:, nqkv + nz + GDN_HEADS:]

    def group_rows(b_part, a_part, dtype):
        rows = b_part.shape[0]
        chunks = []
        for gidx in range(n_groups):
            sl = slice(gidx * GDN_GROUP, (gidx + 1) * GDN_GROUP)
            chunks += [b_part[:, sl], a_part[:, sl], jnp.zeros((rows, LANES - 2 * GDN_GROUP), dtype)]
        return jnp.concatenate(chunks, axis=1)

    w_p = jnp.concatenate([w_in[:, :nqkv + nz], group_rows(wb, wa, F32)], axis=1).astype(BF16)
    zero_h = jnp.zeros((1, GDN_HEADS), F32)
    alog_p = group_rows(zero_h, a_log.reshape(1, GDN_HEADS), F32)
    dtb_p = group_rows(zero_h, dt_bias.reshape(1, GDN_HEADS), F32)
    wgu = w_gate_up.astype(BF16)

    ffn_tile = lambda i: jnp.minimum(i, n_tiles - 1)
    gdn_tile = lambda i: jnp.maximum(i - 1, 0)
    ffn_rows = lambda n: pl.BlockSpec((tm, n), lambda i: (ffn_tile(i), 0))
    gdn_rows = lambda n: pl.BlockSpec((tm, n), lambda i: (gdn_tile(i), 0))
    ffn_mod = lambda which: _mod_spec(0, which, batch, tiles_per_seq, d, ffn_tile)
    gdn_mod = lambda which: _mod_spec(1, which, batch, tiles_per_seq, d, gdn_tile)
    outs = pl.pallas_call(
        functools.partial(_ffn_gdn_kernel, tiles_per_seq=tiles_per_seq, n_chunks=2),
        grid=(n_tiles + 1,),
        in_specs=[
            ffn_rows(kdim), _resident((kdim, d)), ffn_mod(2),
            ffn_rows(d), _resident((1, d)), ffn_mod(4), ffn_mod(3), ffn_mod(5),
            _resident((d, f)), _resident((d, f), (0, 1)), _resident((f, d)),
            _resident((1, d)), gdn_mod(1), gdn_mod(0),
            _resident(w_p.shape), _resident((CONV_K, nqkv)),
            _resident((1, n_groups * LANES)), _resident((1, n_groups * LANES)),
        ],
        out_specs=[ffn_rows(d), gdn_rows(nqk), gdn_rows(nqk), gdn_rows(nz), gdn_rows(nz),
                   gdn_rows(n_groups * LANES)],
        out_shape=[
            jax.ShapeDtypeStruct((t, d), F32),
            jax.ShapeDtypeStruct((t, nqk), BF16),
            jax.ShapeDtypeStruct((t, nqk), BF16),
            jax.ShapeDtypeStruct((t, nz), BF16),
            jax.ShapeDtypeStruct((t, nz), BF16),
            jax.ShapeDtypeStruct((t, n_groups * LANES), F32),
        ],
        scratch_shapes=[pltpu.VMEM((tm, d), F32), pltpu.VMEM((8, nqkv), F32)],
        compiler_params=_params("arbitrary"),
        name="ffn_gdn_proj",
    )(a, w_out.astype(BF16), mod3, x2, gain_f.reshape(1, d), mod3, mod3, mod3, wgu, wgu,
      w_down.astype(BF16), gain_g.reshape(1, d), mod3, mod3, w_p, conv_w, alog_p, dtb_p)
    return outs


def _gdn_core_kernel(q_ref, k_ref, v_ref, z_ref, bg_ref, gn_ref, o_ref, state_ref):
    ts = q_ref.shape[1]
    c = CHUNK
    nc = ts // c
    nh = GDN_HEADS
    row = lax.broadcasted_iota(jnp.int32, (1, c, c), 1)
    col = lax.broadcasted_iota(jnp.int32, (1, c, c), 2)
    causal = col <= row
    strict = col < row
    eye = jnp.where(row == col, 1.0, 0.0).astype(F32)

    def level_mask(s):
        return ((row & -2 * s) == (col & -2 * s)) & ((row & s) != 0) & ((col & s) == 0)

    lane = lax.broadcasted_iota(jnp.int32, (nh * nc, c, LANES), 2)

    @pl.when(pl.program_id(1) == 0)
    def _():
        state_ref[...] = jnp.zeros(state_ref.shape, F32)

    bg = bg_ref[0]

    def stack(ref):
        return jnp.concatenate(
            [ref[0, :, hd * GDN_DK:(hd + 1) * GDN_DK].reshape(nc, c, GDN_DK) for hd in range(nh)], axis=0)

    def stack_col(first_lane):
        return jnp.concatenate(
            [bg[:, first_lane + hd:first_lane + hd + 1].reshape(nc, c, 1) for hd in range(nh)], axis=0)

    q = stack(q_ref).astype(F32)
    k16 = stack(k_ref)
    k = k16.astype(F32)
    v = stack(v_ref).astype(F32)
    beta = stack_col(0)
    gc = stack_col(nh)

    g_hi, g_mid, g_lo = (t.astype(F32) for t in _split3(gc))
    terms = lambda base, other: jnp.where(lane == base, g_hi, jnp.where(
        lane == base + 1, g_mid, jnp.where(lane == base + 2, g_lo, other)))
    lhs = terms(0, jnp.where(lane < 6, 1.0, 0.0)).astype(BF16)
    rhs = terms(3, jnp.where(lane < 3, -1.0, 0.0)).astype(BF16)
    diff = -_bmm_nt(lhs, rhs)
    decay = jnp.where(causal, jnp.exp(jnp.where(causal, diff, 0.0)), 0.0)

    kb = k * beta
    a_mat = jnp.where(strict, _bmm_nt(kb.astype(BF16), k16) * decay, 0.0)
    attn = jnp.where(causal, _bmm_nt(q.astype(BF16), k16) * decay, 0.0).astype(BF16)

    inv = eye - jnp.where(level_mask(1), a_mat, 0.0)
    s = 2
    while s < c:
        x_s = jnp.where(level_mask(s), a_mat, 0.0).astype(BF16)
        inv16 = inv.astype(BF16)
        inv = inv - _bmm(_bmm(inv16, x_s).astype(BF16), inv16)
        s *= 2

    eg = jnp.exp(gc)
    g_last = gc[:, c - 1:c, :]
    rhs_uw = jnp.concatenate([v * beta, kb * eg], axis=-1).astype(BF16)
    uw = _bmm(inv.astype(BF16), rhs_uw)
    a_uw = _bmm(attn, uw.astype(BF16))
    o_intra = a_uw[..., :GDN_DV]
    q_eff = (q * eg - a_uw[..., GDN_DV:]).astype(BF16)
    kt_uw = _bmm_tn(k16, (uw * jnp.exp(g_last - gc)).astype(BF16))
    s_add = kt_uw[..., :GDN_DV]
    s_mul = kt_uw[..., GDN_DV:].astype(BF16)
    dec = jnp.exp(g_last)

    per_head = lambda a: a.reshape((nh, nc) + a.shape[1:])
    q_eff, o_intra, s_add, s_mul, dec = map(per_head, (q_eff, o_intra, s_add, s_mul, dec))
    st = state_ref[...]
    outs = []
    for n in range(nc):
        st16 = st.astype(BF16)
        outs.append(_bmm(q_eff[:, n], st16) + o_intra[:, n])
        st = dec[:, n] * st - _bmm(s_mul[:, n], st16) + s_add[:, n]
    state_ref[...] = st
    o = jnp.stack(outs, axis=1).reshape(nh, ts, GDN_DV)
    gn = gn_ref[...]
    for hd in range(nh):
        sl = slice(hd * GDN_DV, (hd + 1) * GDN_DV)
        z = z_ref[0, :, sl].astype(F32)
        o_ref[0, :, sl] = (_rms(o[hd]) * gn * _silu(z)).astype(BF16)


def _gdn_core(q, k, v, z, bg, out_norm, *, batch, seq, ts):
    width = GDN_HEADS * GDN_DK
    shp = lambda a: a.reshape(batch, seq, a.shape[-1])
    blk = pl.BlockSpec((1, ts, width), lambda b, i: (b, i, 0))
    out = pl.pallas_call(
        _gdn_core_kernel,
        grid=(batch, seq // ts),
        in_specs=[blk, blk, blk, blk,
                  pl.BlockSpec((1, ts, LANES), lambda b, i: (b, i, 0)),
                  pl.BlockSpec((1, GDN_DV), lambda b, i: (0, 0))],
        out_specs=blk,
        out_shape=jax.ShapeDtypeStruct((batch, seq, GDN_HEADS * GDN_DV), BF16),
        scratch_shapes=[pltpu.VMEM((GDN_HEADS, GDN_DK, GDN_DV), F32)],
        compiler_params=_params("parallel", "arbitrary"),
        name="gdn_core",
    )(shp(q), shp(k), shp(v), shp(z), shp(bg), out_norm.reshape(1, GDN_DV))
    return out.reshape(batch * seq, GDN_HEADS * GDN_DV)


def _moe_slots(tb):
    slots = MOE_TOP_K * tb + N_EXPERTS * MOE_GRANULE
    return slots, slots // MOE_GRANULE


def _moe_route_kernel(a_ref, wo_ref, gate1_ref, x_ref, gain_ref, sc_ref, sh_ref, wr_ref,
                      x_out, hs_ref, rt_ref, meta_ref, *, tb):
    n_sub = x_ref.shape[0] // tb
    refs = (a_ref, wo_ref, gate1_ref, x_ref, gain_ref, sc_ref, sh_ref, wr_ref, x_out, hs_ref, rt_ref, meta_ref)
    _round_robin([_moe_route_stages(sub, tb, hs_ref.shape[0] // n_sub, *refs) for sub in range(n_sub)])


def _round_robin(stage_generators):
    end = object()
    live = list(stage_generators)
    while live:
        live = [g for g in live if next(g, end) is not end]


def _moe_route_stages(sub, tb, slots, a_ref, wo_ref, gate1_ref, x_ref, gain_ref, sc_ref, sh_ref, wr_ref,
                      x_out, hs_ref, rt_ref, meta_ref):
    rows = slice(sub * tb, (sub + 1) * tb)
    x = x_ref[rows, :] + gate1_ref[...] * _dot(a_ref[rows, :], wo_ref[...])
    x_out[rows, :] = x
    yield
    h = _mod_norm(x, gain_ref[...], sc_ref[...], sh_ref[...])
    h_hi, h_mid, _ = _split3(h)
    logits = _dot(h_hi, wr_ref[0]) + (_dot(h_hi, wr_ref[1]) + _dot(h_mid, wr_ref[0]))
    yield
    lane = lax.broadcasted_iota(jnp.int32, (tb, LANES), 1)
    lane_f = lane.astype(F32)
    logits = jnp.where(lane < N_EXPERTS, logits, -jnp.inf)
    m1 = jnp.max(logits, axis=-1, keepdims=True)
    i1 = jnp.min(jnp.where(logits == m1, lane_f, float(LANES)), axis=-1, keepdims=True)
    rest = jnp.where(lane_f == i1, -jnp.inf, logits)
    m2 = jnp.max(rest, axis=-1, keepdims=True)
    i2 = jnp.min(jnp.where(rest == m2, lane_f, float(LANES)), axis=-1, keepdims=True)
    e2 = jnp.exp(m2 - m1)
    w1 = 1.0 / (1.0 + e2)
    w2 = e2 * w1

    first = lane_f == i1
    second = lane_f == i2
    chosen = jnp.where(first, 1.0, jnp.where(second, 1.0, 0.0))
    r = lax.broadcasted_iota(jnp.int32, (tb, tb), 0)
    c = lax.broadcasted_iota(jnp.int32, (tb, tb), 1)
    earlier = jnp.where(c < r, 1.0, 0.0).astype(BF16)
    rank = _dot(earlier, chosen.astype(BF16))
    yield
    count = jnp.sum(chosen, axis=0, keepdims=True)
    granules = jnp.floor((count + (MOE_GRANULE - 1)) * (1.0 / MOE_GRANULE))
    er = lax.broadcasted_iota(jnp.int32, (LANES, LANES), 0)
    ec = lax.broadcasted_iota(jnp.int32, (LANES, LANES), 1)
    before = jnp.where(er < ec, 1.0, 0.0).astype(BF16)
    seg_gran = _dot(jnp.broadcast_to(granules, (8, LANES)).astype(BF16), before)[0:1]
    yield
    slot_of = seg_gran * MOE_GRANULE + rank
    pos1 = jnp.sum(jnp.where(first, slot_of, 0.0), axis=-1, keepdims=True)
    pos2 = jnp.sum(jnp.where(second, slot_of, 0.0), axis=-1, keepdims=True)

    slot = lax.broadcasted_iota(jnp.int32, (tb, slots), 1).astype(F32)
    place = jnp.where(slot == pos1, 1.0, jnp.where(slot == pos2, 1.0, 0.0)).astype(BF16)
    hs_ref[sub * slots:(sub + 1) * slots, :] = _dot_tn(place, h_hi).astype(BF16)
    rt_ref[rows, :] = jnp.where(lane == 0, pos1, jnp.where(lane == 1, pos2, jnp.where(
        lane == 2, w1, jnp.where(lane == 3, w2, 0.0))))
    row8 = lax.broadcasted_iota(jnp.int32, (8, LANES), 0)
    meta_ref[sub] = jnp.where(row8 == 0, granules, jnp.where(row8 == 1, seg_gran, 0.0))


def _moe_expert_kernel(src_ref, te_ref, used_ref, *refs):
    n_gran = len(refs) - 4
    wg_ref, wu_ref, wd_ref, o_ref = refs[n_gran:]
    step = pl.program_id(0)

    @pl.when(step < used_ref[0])
    def _():
        h = jnp.concatenate([g[...] for g in refs[:n_gran]], axis=0)
        o_ref[...] = _swiglu(h, wg_ref.at[0], wu_ref.at[0], wd_ref.at[0], 1).astype(BF16)

    @pl.when(step >= used_ref[0])
    def _():
        o_ref[...] = jnp.zeros(o_ref.shape, BF16)


def _moe_combine_kernel(inv_ref, *refs, tb):
    n_gran = len(refs) - 5
    rt_ref, x_ref, gate_ref, fn_ref, o_ref = refs[n_gran:]
    n_sub = x_ref.shape[0] // tb
    per_tile = n_gran // n_sub

    def stages(sub):
        rows = slice(sub * tb, (sub + 1) * tb)
        ys = jnp.concatenate([g[...] for g in refs[sub * per_tile:(sub + 1) * per_tile]], axis=0)
        rt = rt_ref[rows, :]
        pos1, pos2, w1, w2 = rt[:, 0:1], rt[:, 1:2], rt[:, 2:3], rt[:, 3:4]
        slot = lax.broadcasted_iota(jnp.int32, (tb, ys.shape[0]), 1).astype(F32)
        y1 = _dot(jnp.where(slot == pos1, 1.0, 0.0).astype(BF16), ys)
        yield
        y2 = _dot(jnp.where(slot == pos2, 1.0, 0.0).astype(BF16), ys)
        yield
        xo = x_ref[rows, :] + gate_ref[...] * (w1 * y1 + w2 * y2)
        o_ref[rows, :] = _rms(xo) * fn_ref[...]

    _round_robin([stages(sub) for sub in range(n_sub)])


def _moe_tables(meta, n_steps, gran_per_step, gran_per_tile):
    i32 = jnp.int32
    cnt = meta[:, 0, :N_EXPERTS].astype(i32)
    seg = meta[:, 1, :N_EXPERTS].astype(i32)
    n_tiles = cnt.shape[0]
    earlier = jnp.cumsum(cnt, axis=0) - cnt
    steps = (jnp.sum(cnt, axis=0) + gran_per_step - 1) // gran_per_step
    step_end = jnp.cumsum(steps)
    used = step_end[-1:]
    seg_pos = (step_end - steps)[None, :] * gran_per_step + earlier
    seg_src = jnp.arange(n_tiles, dtype=i32)[:, None] * gran_per_tile + seg

    s = jnp.arange(n_steps * gran_per_step, dtype=i32)[:, None]
    pos, length, start = (a.reshape(1, -1) for a in (seg_pos, cnt, seg_src))
    src = jnp.sum(jnp.where((s >= pos) & (s < pos + length), start - pos + s, 0), axis=1)
    st = jnp.arange(n_steps, dtype=i32)[:, None]
    step_expert = jnp.minimum(jnp.sum((step_end[None, :] <= st).astype(i32), axis=1), N_EXPERTS - 1)

    g = jnp.arange(gran_per_tile, dtype=i32)[None, :, None]
    lo, n, base = (a[:, None, :] for a in (seg, cnt, seg_pos))
    inv = jnp.sum(jnp.where((g >= lo) & (g < lo + n), base + g - lo, 0), axis=-1).reshape(-1)
    return src, step_expert, used, inv


def _mixer_moe(a, w_out, x2, mod3, gain, w_router, w_gate_up, w_down, final_norm, *, layer, batch, seq, tb):
    t, d = x2.shape
    kdim = a.shape[1]
    n_e, fe, _ = w_down.shape
    tiles_per_seq = seq // tb
    n_tiles = t // tb
    slots, gran_per_tile = _moe_slots(tb)
    gran_per_step = MOE_LHS_ROWS // MOE_GRANULE
    n_steps = pl.cdiv(MOE_TOP_K * t // MOE_GRANULE + n_tiles * n_e, gran_per_step) + n_e
    wr = jnp.pad(w_router, ((0, 0), (0, LANES - n_e)))
    wr_hi = wr.astype(BF16)
    wr_lo = (wr - wr_hi.astype(F32)).astype(BF16)
    wr2 = jnp.stack([wr_hi, wr_lo])
    wgu = w_gate_up.astype(BF16)
    row_spec = lambda rows, cols: pl.BlockSpec((rows, cols), lambda i, *_: (i, 0))

    per = MOE_ROUTE_TILES
    steps_per_seq = tiles_per_seq // per
    x2, hs, rt, meta = pl.pallas_call(
        functools.partial(_moe_route_kernel, tb=tb),
        grid=(n_tiles // per,),
        in_specs=[
            row_spec(per * tb, kdim), _resident((kdim, d)),
            _mod_spec(layer, 2, batch, steps_per_seq, d),
            row_spec(per * tb, d), _resident((1, d)),
            _mod_spec(layer, 4, batch, steps_per_seq, d),
            _mod_spec(layer, 3, batch, steps_per_seq, d),
            _resident((2, d, LANES)),
        ],
        out_specs=[row_spec(per * tb, d), row_spec(per * slots, d), row_spec(per * tb, LANES),
                   pl.BlockSpec((per, 8, LANES), lambda i: (i, 0, 0))],
        out_shape=[
            jax.ShapeDtypeStruct((t, d), F32),
            jax.ShapeDtypeStruct((n_tiles * slots, d), BF16),
            jax.ShapeDtypeStruct((t, LANES), F32),
            jax.ShapeDtypeStruct((n_tiles, 8, LANES), F32),
        ],
        compiler_params=_params("parallel"),
        name="moe_route",
    )(a, w_out.astype(BF16), mod3, x2, gain.reshape(1, d), mod3, mod3, wr2)

    src, step_expert, used, inv = _moe_tables(meta, n_steps, gran_per_step, gran_per_tile)

    def granule_spec(j, per_step):
        return pl.BlockSpec((MOE_GRANULE, d), lambda i, tbl, *_: (tbl[i * per_step + j], 0))

    ye = pl.pallas_call(
        _moe_expert_kernel,
        grid_spec=pltpu.PrefetchScalarGridSpec(
            num_scalar_prefetch=3,
            grid=(n_steps,),
            in_specs=[granule_spec(j, gran_per_step) for j in range(gran_per_step)] + [
                pl.BlockSpec((1, d, fe), lambda i, src, te, used: (te[i], 0, 0)),
                pl.BlockSpec((1, d, fe), lambda i, src, te, used: (te[i], 0, 1)),
                pl.BlockSpec((1, fe, d), lambda i, src, te, used: (te[i], 0, 0)),
            ],
            out_specs=pl.BlockSpec((MOE_LHS_ROWS, d), lambda i, *_: (i, 0)),
        ),
        out_shape=jax.ShapeDtypeStruct((n_steps * MOE_LHS_ROWS, d), BF16),
        compiler_params=_params("arbitrary"),
        name="moe_experts",
    )(src, step_expert, used, *([hs] * gran_per_step), wgu, wgu, w_down.astype(BF16))

    return pl.pallas_call(
        functools.partial(_moe_combine_kernel, tb=tb),
        grid_spec=pltpu.PrefetchScalarGridSpec(
            num_scalar_prefetch=1,
            grid=(n_tiles // per,),
            in_specs=[granule_spec(j, per * gran_per_tile) for j in range(per * gran_per_tile)] + [
                row_spec(per * tb, LANES), row_spec(per * tb, d),
                _mod_spec(layer, 5, batch, steps_per_seq, d),
                pl.BlockSpec((1, d), lambda i, *_: (0, 0)),
            ],
            out_specs=row_spec(per * tb, d),
        ),
        out_shape=jax.ShapeDtypeStruct((t, d), F32),
        compiler_params=_params("parallel"),
        name="moe_combine",
    )(inv, *([ye] * (per * gran_per_tile)), rt, x2, mod3, final_norm.reshape(1, d))


def kernel(x, c, positions, ada_w, ada_b, norm_mix, norm_ffn, mla_w_in, mla_q_norm, mla_w_qb, mla_kv_norm, mla_w_kvb, mla_w_out, ffn_w_gate_up, ffn_w_down, gdn_w_in, gdn_conv_w, gdn_a_log, gdn_dt_bias, gdn_out_norm, gdn_w_out, moe_w_router, moe_w_gate_up, moe_w_down, final_norm):
    batch, seq, d = x.shape
    depth = ada_w.shape[0]
    assert depth == 2 and seq % 512 == 0
    t = batch * seq
    tm = 512
    dims = dict(batch=batch, seq=seq, tm=tm)

    mod = _adaln(c, ada_w, ada_b)
    mod3 = mod.reshape(depth * batch * N_MOD, 1, d)
    x2 = x.reshape(t, d)
    pos = positions.astype(F32).reshape(t, 1)

    tq = 256
    q, k, vt = _mla_proj(x2, pos, mod3, norm_mix[0], mla_w_in[0], mla_q_norm[0], mla_w_qb[0],
                         mla_kv_norm[0], mla_w_kvb[0], layer=0, tkv=tq, **dims)
    attn = _mla_attention(q, k, vt, batch=batch, seq=seq, tq=tq, n_heads=8)
    x2, gq, gk, gv, gz, bg = _ffn_gdn(attn, mla_w_out[0], x2, mod3, norm_ffn[0], ffn_w_gate_up[0],
                                      ffn_w_down[0], norm_mix[1], gdn_w_in[0], gdn_conv_w[0],
                                      gdn_a_log[0], gdn_dt_bias[0], **dims)
    og = _gdn_core(gq, gk, gv, gz, bg, gdn_out_norm[0], batch=batch, seq=seq, ts=256)
    out = _mixer_moe(og, gdn_w_out[0], x2, mod3, norm_ffn[1], moe_w_router[0], moe_w_gate_up[0],
                     moe_w_down[0], final_norm, layer=1, batch=batch, seq=seq, tb=256)
    return out.reshape(batch, seq, d)
```

```python
import functools
import math

import jax
import jax.numpy as jnp
from jax import lax
from jax.experimental import pallas as pl
from jax.experimental.pallas import tpu as pltpu

F32 = jnp.float32
BF16 = jnp.bfloat16

EPS = 1e-6
N_MOD = 6
LANES = 128
VMEM_LIMIT_BYTES = 60 * 1024 * 1024

MLA_HEADS = 16
Q_LORA = 512
KV_LORA = 256
QK_NOPE = 64
QK_ROPE = 32
V_HEAD = 64
ROPE_THETA = 10000.0
MASK_VALUE = -1e30
ATTN_SUM_ROWS = 16
GDN_HEADS = 8
GDN_DK = 128
GDN_DV = 128
CONV_K = 4
CHUNK = 64
GDN_GROUP = GDN_HEADS
GDN_PROJ_COLS = 256
N_EXPERTS = 8
MOE_TOP_K = 2
MOE_GRANULE = 16
MOE_LHS_ROWS = 512
MOE_ROUTE_TILES = 4


def _params(*sem):
    return pltpu.CompilerParams(dimension_semantics=sem, vmem_limit_bytes=VMEM_LIMIT_BYTES)


def _resident(shape, index=None):
    index = (0,) * len(shape) if index is None else index
    return pl.BlockSpec(shape, lambda *_: index, pipeline_mode=pl.Buffered(1))


def _split3(a):
    hi = a.astype(BF16)
    r = a - hi.astype(F32)
    mid = r.astype(BF16)
    lo = (r - mid.astype(F32)).astype(BF16)
    return hi, mid, lo


def _dot(a, b):
    return jnp.dot(a, b, preferred_element_type=F32)


def _dot_nt(a, b):
    return lax.dot_general(a, b, (((1,), (1,)), ((), ())), preferred_element_type=F32)


def _dot_tn(a, b):
    return lax.dot_general(a, b, (((0,), (0,)), ((), ())), preferred_element_type=F32)


def _bmm(a, b):
    return lax.dot_general(a, b, (((2,), (1,)), ((0,), (0,))), preferred_element_type=F32)


def _bmm_nt(a, b):
    return lax.dot_general(a, b, (((2,), (2,)), ((0,), (0,))), preferred_element_type=F32)


def _bmm_tn(a, b):
    return lax.dot_general(a, b, (((1,), (1,)), ((0,), (0,))), preferred_element_type=F32)


def _dot_f32(a, b):
    a_hi, a_mid, _ = _split3(a)
    b_hi, b_mid, _ = _split3(b)
    return _dot(a_hi, b_hi) + (_dot(a_hi, b_mid) + _dot(a_mid, b_hi))


def _rms(x):
    return x * lax.rsqrt(jnp.mean(x * x, axis=-1, keepdims=True) + EPS)


def _silu(x):
    return x * jax.nn.sigmoid(x)


def _mod_norm(x, gain, sc, sh):
    return _rms(x) * gain * (1.0 + sc) + sh


def _adaln_kernel(c_ref, w_ref, b_ref, o_ref):
    c = c_ref[...]
    cond = _silu(c)
    o_ref[0] = _dot_f32(cond, w_ref[0]) + b_ref[0]


def _adaln(c, ada_w, ada_b):
    depth, d, n = ada_w.shape
    b = c.shape[0]
    tn = d
    return pl.pallas_call(
        _adaln_kernel,
        grid=(depth, n // tn),
        in_specs=[
            pl.BlockSpec((b, d), lambda l, j: (0, 0)),
            pl.BlockSpec((1, d, tn), lambda l, j: (l, 0, j)),
            pl.BlockSpec((1, 1, tn), lambda l, j: (l, 0, j)),
        ],
        out_specs=pl.BlockSpec((1, b, tn), lambda l, j: (l, 0, j)),
        out_shape=jax.ShapeDtypeStruct((depth, b, n), F32),
        compiler_params=_params("parallel", "parallel"),
        name="adaln",
    )(c, ada_w, ada_b.reshape(depth, 1, n))


def _mod_spec(layer, which, batch, tiles_per_seq, d, tile_of=lambda i: i):
    def index(i, *_):
        return ((layer * batch + tile_of(i) // tiles_per_seq) * N_MOD + which, 0, 0)
    return pl.BlockSpec((None, 1, d), index)


def _mla_proj_kernel(x_ref, pos_ref, gain_ref, sc_ref, sh_ref, win_ref, qn_ref, wqb_ref,
                     kvn_ref, wk_ref, wv_ref, freq_ref, sign_ref, q_out, k_out, vt_out, *, scale):
    tm = x_ref.shape[0]
    h = _mod_norm(x_ref[...], gain_ref[...], sc_ref[...], sh_ref[...]).astype(BF16)
    proj = _dot(h, win_ref[...])
    q_lat = proj[:, :Q_LORA]
    kv_lat = proj[:, Q_LORA:Q_LORA + KV_LORA]
    k_rope = proj[:, Q_LORA + KV_LORA:]
    qn = (_rms(q_lat) * qn_ref[...]).astype(BF16)
    kvn = (_rms(kv_lat) * kvn_ref[...]).astype(BF16)

    ang = pos_ref[...] * freq_ref[...]
    cos = jnp.cos(ang)
    sin = jnp.sin(ang) * sign_ref[...]
    lane = lax.broadcasted_iota(jnp.int32, (tm, LANES), 1)
    first_half = lane < QK_NOPE + QK_ROPE // 2

    def rope(t, cos, sin):
        partner = jnp.where(first_half, pltpu.roll(t, LANES - QK_ROPE // 2, 1),
                            pltpu.roll(t, QK_ROPE // 2, 1))
        return t * cos + partner * sin

    q = _dot(qn, wqb_ref[...])
    k = _dot(kvn, wk_ref[...])
    k_rope = rope(k_rope, cos, sin)
    cos_q = cos * scale
    sin_q = sin * scale
    for hd in range(MLA_HEADS):
        sl = slice(hd * LANES, (hd + 1) * LANES)
        q_out[:, sl] = rope(q[:, sl], cos_q, sin_q).astype(BF16)
        k_out[:, sl] = (k[:, sl] + k_rope).astype(BF16)
    vt = _dot(kvn, wv_ref[...]).T
    tkv = vt_out.shape[2]
    for j in range(vt_out.shape[0]):
        vt_out[j] = vt[:, j * tkv:(j + 1) * tkv].astype(BF16)


def _mla_proj(x2, pos, mod3, gain, w_in, q_norm, w_qb, kv_norm, w_kvb, *, layer, batch, seq, tm, tkv):
    t, d = x2.shape
    tiles_per_seq = seq // tm
    hq = MLA_HEADS * LANES
    pad_in = jnp.zeros((d, LANES), F32).at[:, QK_NOPE:QK_NOPE + QK_ROPE].set(w_in[:, Q_LORA + KV_LORA:])
    w_in_p = jnp.concatenate([w_in[:, :Q_LORA + KV_LORA], pad_in], axis=1).astype(BF16)
    w_qb_p = jnp.pad(w_qb.reshape(Q_LORA, MLA_HEADS, QK_NOPE + QK_ROPE),
                     ((0, 0), (0, 0), (0, LANES - QK_NOPE - QK_ROPE))).reshape(Q_LORA, hq).astype(BF16)
    w_kv3 = w_kvb.reshape(KV_LORA, MLA_HEADS, QK_NOPE + V_HEAD)
    w_k_p = jnp.pad(w_kv3[:, :, :QK_NOPE], ((0, 0), (0, 0), (0, LANES - QK_NOPE))
                    ).reshape(KV_LORA, hq).astype(BF16)
    w_v = w_kv3[:, :, QK_NOPE:].reshape(KV_LORA, MLA_HEADS * V_HEAD).astype(BF16)
    half = QK_ROPE // 2
    inv_freq = ROPE_THETA ** (-jnp.arange(half, dtype=F32) / half)
    zeros = jnp.zeros((QK_NOPE,), F32)
    tail = jnp.zeros((LANES - QK_NOPE - QK_ROPE,), F32)
    freq = jnp.concatenate([zeros, inv_freq, inv_freq, tail]).reshape(1, LANES)
    sign = jnp.concatenate([zeros, -jnp.ones((half,), F32), jnp.ones((half,), F32), tail]).reshape(1, LANES)
    scale = float(QK_NOPE + QK_ROPE) ** -0.5 * math.log2(math.e)

    const = _resident
    return pl.pallas_call(
        functools.partial(_mla_proj_kernel, scale=scale),
        grid=(t // tm,),
        in_specs=[
            pl.BlockSpec((tm, d), lambda i: (i, 0)),
            pl.BlockSpec((tm, 1), lambda i: (i, 0)),
            const((1, d)),
            _mod_spec(layer, 1, batch, tiles_per_seq, d),
            _mod_spec(layer, 0, batch, tiles_per_seq, d),
            const(w_in_p.shape), const((1, Q_LORA)), const(w_qb_p.shape),
            const((1, KV_LORA)), const(w_k_p.shape), const(w_v.shape),
            const((1, LANES)), const((1, LANES)),
        ],
        out_specs=[
            pl.BlockSpec((tm, hq), lambda i: (i, 0)),
            pl.BlockSpec((tm, hq), lambda i: (i, 0)),
            pl.BlockSpec((tm // tkv, MLA_HEADS * V_HEAD, tkv), lambda i: (i, 0, 0)),
        ],
        out_shape=[
            jax.ShapeDtypeStruct((t, hq), BF16),
            jax.ShapeDtypeStruct((t, hq), BF16),
            jax.ShapeDtypeStruct((t // tkv, MLA_HEADS * V_HEAD, tkv), BF16),
        ],
        compiler_params=_params("parallel"),
        name="mla_proj",
    )(x2, pos, gain.reshape(1, d), mod3, mod3, w_in_p, q_norm.reshape(1, Q_LORA), w_qb_p,
      kv_norm.reshape(1, KV_LORA), w_k_p, w_v, freq, sign)


def _attn_kernel(q_ref, k_ref, vt_ref, o_ref, s_a, s_b, p_a, p_b, acc_ref, stat_ref, *, tq, n_heads):
    qi = pl.program_id(2)
    kv_idx = lax.broadcasted_iota(jnp.int32, (tq, tq), 0)
    q_idx = lax.broadcasted_iota(jnp.int32, (tq, tq), 1)
    causal = kv_idx <= q_idx
    heads = range(n_heads)

    def scores_into(j, s_ref):
        r0 = pl.multiple_of(j * tq, tq)
        for h in heads:
            q = q_ref[0, :, h * LANES:(h + 1) * LANES]
            s_ref[h] = _dot_nt(k_ref[0, pl.ds(r0, tq), h * LANES:(h + 1) * LANES], q)

    ones_rows = jnp.ones((ATTN_SUM_ROWS, tq), BF16)

    def values(j, p):
        return [_dot(jnp.concatenate([vt_ref[j, h * V_HEAD:(h + 1) * V_HEAD, :], ones_rows], axis=0), p(h))
                for h in heads]

    def softmax(s_ref, masked):
        p_all = []
        for h in heads:
            s = s_ref[h]
            if masked:
                s = jnp.where(causal, s, MASK_VALUE)
            m_old = stat_ref[h, 0:1, :]
            m_new = jnp.maximum(m_old, jnp.max(s, axis=0, keepdims=True))
            p = jnp.exp2(s - m_new)
            alpha = jnp.exp2(m_old - m_new)
            stat_ref[h, 0:1, :] = m_new
            stat_ref[h, 1:2, :] = alpha
            p_all.append(p.astype(BF16))
        return p_all

    def step(j, s_cur, p_cur, s_next, p_prev):
        pv = values(jnp.maximum(j - 1, 0), lambda h: p_prev[h])
        alpha_prev = [stat_ref[h, 1:2, :] for h in heads]
        scores_into(j + 1, s_next)
        p = softmax(s_cur, masked=False)
        for h in heads:
            p_cur[h] = p[h]
            acc_ref[h] = alpha_prev[h] * acc_ref[h] + pv[h]

    def finish(s_cur, p_prev):
        pv = values(jnp.maximum(qi - 1, 0), lambda h: p_prev[h])
        alpha_prev = [stat_ref[h, 1:2, :] for h in heads]
        p = softmax(s_cur, masked=True)
        pv_last = values(qi, lambda h: p[h])
        out = []
        for h in heads:
            acc = stat_ref[h, 1:2, :] * (alpha_prev[h] * acc_ref[h] + pv[h]) + pv_last[h]
            out.append(acc[:V_HEAD] / acc[V_HEAD:V_HEAD + 1])
        for pair in range(n_heads // 2):
            o_t = jnp.concatenate(out[2 * pair:2 * pair + 2], axis=0)
            o_ref[0, :, pair * LANES:(pair + 1) * LANES] = o_t.T.astype(BF16)

    row3 = lax.broadcasted_iota(jnp.int32, stat_ref.shape, 1)
    stat_ref[...] = jnp.where(row3 == 0, MASK_VALUE, 1.0)
    acc_ref[...] = jnp.zeros(acc_ref.shape, F32)
    p_b[...] = jnp.zeros(p_b.shape, BF16)
    scores_into(0, s_a)

    def pair_of_steps(jj, _):
        step(2 * jj, s_a, p_a, s_b, p_b)
        step(2 * jj + 1, s_b, p_b, s_a, p_a)
        return 0

    lax.fori_loop(0, qi // 2, pair_of_steps, 0)

    @pl.when(qi % 2 == 1)
    def _():
        step(qi - 1, s_a, p_a, s_b, p_b)
        finish(s_b, p_a)

    @pl.when(qi % 2 == 0)
    def _():
        finish(s_a, p_b)


def _mla_attention(q, k, vt, *, batch, seq, tq, n_heads):
    hq = MLA_HEADS * LANES
    q3 = q.reshape(batch, seq, hq)
    k3 = k.reshape(batch, seq, hq)
    n_kv = seq // tq
    out = pl.pallas_call(
        functools.partial(_attn_kernel, tq=tq, n_heads=n_heads),
        grid=(batch, MLA_HEADS // n_heads, seq // tq),
        in_specs=[
            pl.BlockSpec((1, tq, n_heads * LANES), lambda b, h, i: (b, i, h)),
            pl.BlockSpec((1, seq, n_heads * LANES), lambda b, h, i: (b, 0, h)),
            pl.BlockSpec((n_kv, n_heads * V_HEAD, tq), lambda b, h, i: (b, h, 0)),
        ],
        out_specs=pl.BlockSpec((1, tq, n_heads * V_HEAD), lambda b, h, i: (b, i, h)),
        out_shape=jax.ShapeDtypeStruct((batch, seq, MLA_HEADS * V_HEAD), BF16),
        scratch_shapes=[
            pltpu.VMEM((n_heads, tq, tq), F32), pltpu.VMEM((n_heads, tq, tq), F32),
            pltpu.VMEM((n_heads, tq, tq), BF16), pltpu.VMEM((n_heads, tq, tq), BF16),
            pltpu.VMEM((n_heads, V_HEAD + ATTN_SUM_ROWS, tq), F32), pltpu.VMEM((n_heads, 8, tq), F32),
        ],
        compiler_params=_params("parallel", "parallel", "parallel"),
        name="mla_attn",
    )(q3, k3, vt)
    return out.reshape(batch * seq, MLA_HEADS * V_HEAD)


def _swiglu(h, wg_ref, wu_ref, wd_ref, n_chunks):
    f = wd_ref.shape[0]
    tf = f // n_chunks
    acc = None
    for j in range(n_chunks):
        sl = slice(j * tf, (j + 1) * tf)
        g = _dot(h, wg_ref[:, sl])
        u = _dot(h, wu_ref[:, sl])
        part = _dot((_silu(g) * u).astype(BF16), wd_ref[sl, :])
        acc = part if acc is None else acc + part
    return acc


def _ffn_gdn_kernel(a_ref, wo_ref, gate1_ref, x_ref, gain_f, sc_f, sh_f, gate2_ref, wg_ref, wu_ref, wd_ref,
                    gain_g, sc_g, sh_g, w_ref, conv_ref, alog_ref, dtb_ref,
                    x_out, q_out, k_out, v_out, z_out, bg_out, xprev_ref, tail_ref, *, tiles_per_seq, n_chunks):
    i = pl.program_id(0)

    @pl.when(i == 0)
    def _():
        xprev_ref[...] = jnp.zeros(xprev_ref.shape, F32)

    @pl.when((i - 1) % tiles_per_seq == 0)
    def _():
        tail_ref[...] = jnp.zeros(tail_ref.shape, F32)

    _gdn_proj_tile(xprev_ref[...], gain_g, sc_g, sh_g, w_ref, conv_ref, alog_ref, dtb_ref,
                   q_out, k_out, v_out, z_out, bg_out, tail_ref)

    x = x_ref[...] + gate1_ref[...] * _dot(a_ref[...], wo_ref[...])
    h = _mod_norm(x, gain_f[...], sc_f[...], sh_f[...]).astype(BF16)
    x = x + gate2_ref[...] * _swiglu(h, wg_ref, wu_ref, wd_ref, n_chunks)
    x_out[...] = x
    xprev_ref[...] = x


def _gdn_proj_tile(x, gain_ref, sc_ref, sh_ref, w_ref, conv_ref, alog_ref, dtb_ref,
                   q_out, k_out, v_out, z_out, bg_out, tail_ref):
    tm = x.shape[0]
    nqk = GDN_HEADS * GDN_DK
    nqkv = 2 * nqk + GDN_HEADS * GDN_DV
    nz = GDN_HEADS * GDN_DV
    h = _mod_norm(x, gain_ref[...], sc_ref[...], sh_ref[...]).astype(BF16)

    def l2n(t):
        return t * lax.rsqrt(jnp.sum(t * t, axis=-1, keepdims=True) + EPS)

    for c0 in range(0, nqkv, GDN_PROJ_COLS):
        cols = slice(c0, c0 + GDN_PROJ_COLS)
        cur = _dot(h, w_ref[:, cols])
        ext = jnp.concatenate([tail_ref[:, cols], cur], axis=0)
        tail_ref[:, cols] = cur[tm - 8:, :]
        y = conv_ref[CONV_K - 1:CONV_K, cols] * cur
        for back in range(1, CONV_K):
            tap = CONV_K - 1 - back
            y = y + conv_ref[tap:tap + 1, cols] * pltpu.roll(ext, back, 0)[8:, :]
        act = _silu(y)
        for h0 in range(0, GDN_PROJ_COLS, GDN_DK):
            head = act[:, h0:h0 + GDN_DK]
            dst = slice(c0 % nqk + h0, c0 % nqk + h0 + GDN_DK)
            if c0 < nqk:
                q_out[:, dst] = (l2n(head) * (GDN_DK ** -0.5)).astype(BF16)
            elif c0 < 2 * nqk:
                k_out[:, dst] = l2n(head).astype(BF16)
            else:
                v_out[:, dst] = head.astype(BF16)
    for c0 in range(0, nz, GDN_PROJ_COLS):
        z_out[:, c0:c0 + GDN_PROJ_COLS] = _dot(h, w_ref[:, nqkv + c0:nqkv + c0 + GDN_PROJ_COLS]).astype(BF16)

    ba = _dot(h, w_ref[:, nqkv + nz:])
    lane = lax.broadcasted_iota(jnp.int32, ba.shape, 1) & (LANES - 1)
    beta = jax.nn.sigmoid(ba)
    pre = ba + dtb_ref[...]
    softplus = jnp.maximum(pre, 0.0) + jnp.log(1.0 + jnp.exp(-jnp.abs(pre)))
    g = jnp.where((lane >= GDN_GROUP) & (lane < 2 * GDN_GROUP), -jnp.exp(alog_ref[...]) * softplus, 0.0)
    row = lax.broadcasted_iota(jnp.int32, (tm, tm), 0)
    col = lax.broadcasted_iota(jnp.int32, (tm, tm), 1)
    same_chunk = (row & -CHUNK) == (col & -CHUNK)
    tril = jnp.where(same_chunk & (col <= row), 1.0, 0.0).astype(BF16)
    g_hi, g_mid, g_lo = _split3(g)
    gc = _dot(tril, g_hi) + (_dot(tril, g_mid) + _dot(tril, g_lo))
    bg_out[...] = jnp.where(lane < GDN_GROUP, beta, gc)


def _ffn_gdn(a, w_out, x2, mod3, gain_f, w_gate_up, w_down, gain_g, w_in, conv_w, a_log, dt_bias,
             *, batch, seq, tm):
    t, d = x2.shape
    f = w_down.shape[0]
    kdim = a.shape[1]
    nqk = GDN_HEADS * GDN_DK
    nqkv = 2 * nqk + GDN_HEADS * GDN_DV
    nz = GDN_HEADS * GDN_DV
    n_groups = GDN_HEADS // GDN_GROUP
    tiles_per_seq = seq // tm
    n_tiles = t // tm
    wb = w_in[:, nqkv + nz:nqkv + nz + GDN_HEADS]
    wa = w_in[:, nqkv + nz + GDN_HEADS:]

    def group_rows(b_part, a_part, dtype):
        rows = b_part.shape[0]
        chunks = []
        for gidx in range(n_groups):
            sl = slice(gidx * GDN_GROUP, (gidx + 1) * GDN_GROUP)
            chunks += [b_part[:, sl], a_part[:, sl], jnp.zeros((rows, LANES - 2 * GDN_GROUP), dtype)]
        return jnp.concatenate(chunks, axis=1)

    w_p = jnp.concatenate([w_in[:, :nqkv + nz], group_rows(wb, wa, F32)], axis=1).astype(BF16)
    zero_h = jnp.zeros((1, GDN_HEADS), F32)
    alog_p = group_rows(zero_h, a_log.reshape(1, GDN_HEADS), F32)
    dtb_p = group_rows(zero_h, dt_bias.reshape(1, GDN_HEADS), F32)
    wgu = w_gate_up.astype(BF16)

    ffn_tile = lambda i: jnp.minimum(i, n_tiles - 1)
    gdn_tile = lambda i: jnp.maximum(i - 1, 0)
    ffn_rows = lambda n: pl.BlockSpec((tm, n), lambda i: (ffn_tile(i), 0))
    gdn_rows = lambda n: pl.BlockSpec((tm, n), lambda i: (gdn_tile(i), 0))
    ffn_mod = lambda which: _mod_spec(0, which, batch, tiles_per_seq, d, ffn_tile)
    gdn_mod = lambda which: _mod_spec(1, which, batch, tiles_per_seq, d, gdn_tile)
    outs = pl.pallas_call(
        functools.partial(_ffn_gdn_kernel, tiles_per_seq=tiles_per_seq, n_chunks=2),
        grid=(n_tiles + 1,),
        in_specs=[
            ffn_rows(kdim), _resident((kdim, d)), ffn_mod(2),
            ffn_rows(d), _resident((1, d)), ffn_mod(4), ffn_mod(3), ffn_mod(5),
            _resident((d, f)), _resident((d, f), (0, 1)), _resident((f, d)),
            _resident((1, d)), gdn_mod(1), gdn_mod(0),
            _resident(w_p.shape), _resident((CONV_K, nqkv)),
            _resident((1, n_groups * LANES)), _resident((1, n_groups * LANES)),
        ],
        out_specs=[ffn_rows(d), gdn_rows(nqk), gdn_rows(nqk), gdn_rows(nz), gdn_rows(nz),
                   gdn_rows(n_groups * LANES)],
        out_shape=[
            jax.ShapeDtypeStruct((t, d), F32),
            jax.ShapeDtypeStruct((t, nqk), BF16),
            jax.ShapeDtypeStruct((t, nqk), BF16),
            jax.ShapeDtypeStruct((t, nz), BF16),
            jax.ShapeDtypeStruct((t, nz), BF16),
            jax.ShapeDtypeStruct((t, n_groups * LANES), F32),
        ],
        scratch_shapes=[pltpu.VMEM((tm, d), F32), pltpu.VMEM((8, nqkv), F32)],
        compiler_params=_params("arbitrary"),
        name="ffn_gdn_proj",
    )(a, w_out.astype(BF16), mod3, x2, gain_f.reshape(1, d), mod3, mod3, mod3, wgu, wgu,
      w_down.astype(BF16), gain_g.reshape(1, d), mod3, mod3, w_p, conv_w, alog_p, dtb_p)
    return outs


def _gdn_core_kernel(q_ref, k_ref, v_ref, z_ref, bg_ref, gn_ref, o_ref, state_ref):
    ts = q_ref.shape[1]
    c = CHUNK
    nc = ts // c
    nh = GDN_HEADS
    row = lax.broadcasted_iota(jnp.int32, (1, c, c), 1)
    col = lax.broadcasted_iota(jnp.int32, (1, c, c), 2)
    causal = col <= row
    strict = col < row
    eye = jnp.where(row == col, 1.0, 0.0).astype(F32)

    def level_mask(s):
        return ((row & -2 * s) == (col & -2 * s)) & ((row & s) != 0) & ((col & s) == 0)

    lane = lax.broadcasted_iota(jnp.int32, (nh * nc, c, LANES), 2)

    @pl.when(pl.program_id(1) == 0)
    def _():
        state_ref[...] = jnp.zeros(state_ref.shape, F32)

    bg = bg_ref[0]

    def stack(ref):
        return jnp.concatenate(
            [ref[0, :, hd * GDN_DK:(hd + 1) * GDN_DK].reshape(nc, c, GDN_DK) for hd in range(nh)], axis=0)

    def stack_col(first_lane):
        return jnp.concatenate(
            [bg[:, first_lane + hd:first_lane + hd + 1].reshape(nc, c, 1) for hd in range(nh)], axis=0)

    q = stack(q_ref).astype(F32)
    k16 = stack(k_ref)
    k = k16.astype(F32)
    v = stack(v_ref).astype(F32)
    beta = stack_col(0)
    gc = stack_col(nh)

    g_hi, g_mid, g_lo = (t.astype(F32) for t in _split3(gc))
    terms = lambda base, other: jnp.where(lane == base, g_hi, jnp.where(
        lane == base + 1, g_mid, jnp.where(lane == base + 2, g_lo, other)))
    lhs = terms(0, jnp.where(lane < 6, 1.0, 0.0)).astype(BF16)
    rhs = terms(3, jnp.where(lane < 3, -1.0, 0.0)).astype(BF16)
    diff = -_bmm_nt(lhs, rhs)
    decay = jnp.where(causal, jnp.exp(jnp.where(causal, diff, 0.0)), 0.0)

    kb = k * beta
    a_mat = jnp.where(strict, _bmm_nt(kb.astype(BF16), k16) * decay, 0.0)
    attn = jnp.where(causal, _bmm_nt(q.astype(BF16), k16) * decay, 0.0).astype(BF16)

    inv = eye - jnp.where(level_mask(1), a_mat, 0.0)
    s = 2
    while s < c:
        x_s = jnp.where(level_mask(s), a_mat, 0.0).astype(BF16)
        inv16 = inv.astype(BF16)
        inv = inv - _bmm(_bmm(inv16, x_s).astype(BF16), inv16)
        s *= 2

    eg = jnp.exp(gc)
    g_last = gc[:, c - 1:c, :]
    rhs_uw = jnp.concatenate([v * beta, kb * eg], axis=-1).astype(BF16)
    uw = _bmm(inv.astype(BF16), rhs_uw)
    a_uw = _bmm(attn, uw.astype(BF16))
    o_intra = a_uw[..., :GDN_DV]
    q_eff = (q * eg - a_uw[..., GDN_DV:]).astype(BF16)
    kt_uw = _bmm_tn(k16, (uw * jnp.exp(g_last - gc)).astype(BF16))
    s_add = kt_uw[..., :GDN_DV]
    s_mul = kt_uw[..., GDN_DV:].astype(BF16)
    dec = jnp.exp(g_last)

    per_head = lambda a: a.reshape((nh, nc) + a.shape[1:])
    q_eff, o_intra, s_add, s_mul, dec = map(per_head, (q_eff, o_intra, s_add, s_mul, dec))
    st = state_ref[...]
    outs = []
    for n in range(nc):
        st16 = st.astype(BF16)
        outs.append(_bmm(q_eff[:, n], st16) + o_intra[:, n])
        st = dec[:, n] * st - _bmm(s_mul[:, n], st16) + s_add[:, n]
    state_ref[...] = st
    o = jnp.stack(outs, axis=1).reshape(nh, ts, GDN_DV)
    gn = gn_ref[...]
    for hd in range(nh):
        sl = slice(hd * GDN_DV, (hd + 1) * GDN_DV)
        z = z_ref[0, :, sl].astype(F32)
        o_ref[0, :, sl] = (_rms(o[hd]) * gn * _silu(z)).astype(BF16)


def _gdn_core(q, k, v, z, bg, out_norm, *, batch, seq, ts):
    width = GDN_HEADS * GDN_DK
    shp = lambda a: a.reshape(batch, seq, a.shape[-1])
    blk = pl.BlockSpec((1, ts, width), lambda b, i: (b, i, 0))
    out = pl.pallas_call(
        _gdn_core_kernel,
        grid=(batch, seq // ts),
        in_specs=[blk, blk, blk, blk,
                  pl.BlockSpec((1, ts, LANES), lambda b, i: (b, i, 0)),
                  pl.BlockSpec((1, GDN_DV), lambda b, i: (0, 0))],
        out_specs=blk,
        out_shape=jax.ShapeDtypeStruct((batch, seq, GDN_HEADS * GDN_DV), BF16),
        scratch_shapes=[pltpu.VMEM((GDN_HEADS, GDN_DK, GDN_DV), F32)],
        compiler_params=_params("parallel", "arbitrary"),
        name="gdn_core",
    )(shp(q), shp(k), shp(v), shp(z), shp(bg), out_norm.reshape(1, GDN_DV))
    return out.reshape(batch * seq, GDN_HEADS * GDN_DV)


def _moe_slots(tb):
    slots = MOE_TOP_K * tb + N_EXPERTS * MOE_GRANULE
    return slots, slots // MOE_GRANULE


def _moe_route_kernel(a_ref, wo_ref, gate1_ref, x_ref, gain_ref, sc_ref, sh_ref, wr_ref,
                      x_out, hs_ref, rt_ref, meta_ref, *, tb):
    n_sub = x_ref.shape[0] // tb
    refs = (a_ref, wo_ref, gate1_ref, x_ref, gain_ref, sc_ref, sh_ref, wr_ref, x_out, hs_ref, rt_ref, meta_ref)
    _round_robin([_moe_route_stages(sub, tb, hs_ref.shape[0] // n_sub, *refs) for sub in range(n_sub)])


def _round_robin(stage_generators):
    end = object()
    live = list(stage_generators)
    while live:
        live = [g for g in live if next(g, end) is not end]


def _moe_route_stages(sub, tb, slots, a_ref, wo_ref, gate1_ref, x_ref, gain_ref, sc_ref, sh_ref, wr_ref,
                      x_out, hs_ref, rt_ref, meta_ref):
    rows = slice(sub * tb, (sub + 1) * tb)
    x = x_ref[rows, :] + gate1_ref[...] * _dot(a_ref[rows, :], wo_ref[...])
    x_out[rows, :] = x
    yield
    h = _mod_norm(x, gain_ref[...], sc_ref[...], sh_ref[...])
    h_hi, h_mid, _ = _split3(h)
    logits = _dot(h_hi, wr_ref[0]) + (_dot(h_hi, wr_ref[1]) + _dot(h_mid, wr_ref[0]))
    yield
    lane = lax.broadcasted_iota(jnp.int32, (tb, LANES), 1)
    lane_f = lane.astype(F32)
    logits = jnp.where(lane < N_EXPERTS, logits, -jnp.inf)
    m1 = jnp.max(logits, axis=-1, keepdims=True)
    i1 = jnp.min(jnp.where(logits == m1, lane_f, float(LANES)), axis=-1, keepdims=True)
    rest = jnp.where(lane_f == i1, -jnp.inf, logits)
    m2 = jnp.max(rest, axis=-1, keepdims=True)
    i2 = jnp.min(jnp.where(rest == m2, lane_f, float(LANES)), axis=-1, keepdims=True)
    e2 = jnp.exp(m2 - m1)
    w1 = 1.0 / (1.0 + e2)
    w2 = e2 * w1

    first = lane_f == i1
    second = lane_f == i2
    chosen = jnp.where(first, 1.0, jnp.where(second, 1.0, 0.0))
    r = lax.broadcasted_iota(jnp.int32, (tb, tb), 0)
    c = lax.broadcasted_iota(jnp.int32, (tb, tb), 1)
    earlier = jnp.where(c < r, 1.0, 0.0).astype(BF16)
    rank = _dot(earlier, chosen.astype(BF16))
    yield
    count = jnp.sum(chosen, axis=0, keepdims=True)
    granules = jnp.floor((count + (MOE_GRANULE - 1)) * (1.0 / MOE_GRANULE))
    er = lax.broadcasted_iota(jnp.int32, (LANES, LANES), 0)
    ec = lax.broadcasted_iota(jnp.int32, (LANES, LANES), 1)
    before = jnp.where(er < ec, 1.0, 0.0).astype(BF16)
    seg_gran = _dot(jnp.broadcast_to(granules, (8, LANES)).astype(BF16), before)[0:1]
    yield
    slot_of = seg_gran * MOE_GRANULE + rank
    pos1 = jnp.sum(jnp.where(first, slot_of, 0.0), axis=-1, keepdims=True)
    pos2 = jnp.sum(jnp.where(second, slot_of, 0.0), axis=-1, keepdims=True)

    slot = lax.broadcasted_iota(jnp.int32, (tb, slots), 1).astype(F32)
    place = jnp.where(slot == pos1, 1.0, jnp.where(slot == pos2, 1.0, 0.0)).astype(BF16)
    hs_ref[sub * slots:(sub + 1) * slots, :] = _dot_tn(place, h_hi).astype(BF16)
    rt_ref[rows, :] = jnp.where(lane == 0, pos1, jnp.where(lane == 1, pos2, jnp.where(
        lane == 2, w1, jnp.where(lane == 3, w2, 0.0))))
    row8 = lax.broadcasted_iota(jnp.int32, (8, LANES), 0)
    meta_ref[sub] = jnp.where(row8 == 0, granules, jnp.where(row8 == 1, seg_gran, 0.0))


def _moe_expert_kernel(src_ref, te_ref, used_ref, *refs):
    n_gran = len(refs) - 4
    wg_ref, wu_ref, wd_ref, o_ref = refs[n_gran:]
    step = pl.program_id(0)

    @pl.when(step < used_ref[0])
    def _():
        h = jnp.concatenate([g[...] for g in refs[:n_gran]], axis=0)
        o_ref[...] = _swiglu(h, wg_ref.at[0], wu_ref.at[0], wd_ref.at[0], 1).astype(BF16)

    @pl.when(step >= used_ref[0])
    def _():
        o_ref[...] = jnp.zeros(o_ref.shape, BF16)


def _moe_combine_kernel(inv_ref, *refs, tb):
    n_gran = len(refs) - 5
    rt_ref, x_ref, gate_ref, fn_ref, o_ref = refs[n_gran:]
    n_sub = x_ref.shape[0] // tb
    per_tile = n_gran // n_sub

    def stages(sub):
        rows = slice(sub * tb, (sub + 1) * tb)
        ys = jnp.concatenate([g[...] for g in refs[sub * per_tile:(sub + 1) * per_tile]], axis=0)
        rt = rt_ref[rows, :]
        pos1, pos2, w1, w2 = rt[:, 0:1], rt[:, 1:2], rt[:, 2:3], rt[:, 3:4]
        slot = lax.broadcasted_iota(jnp.int32, (tb, ys.shape[0]), 1).astype(F32)
        y1 = _dot(jnp.where(slot == pos1, 1.0, 0.0).astype(BF16), ys)
        yield
        y2 = _dot(jnp.where(slot == pos2, 1.0, 0.0).astype(BF16), ys)
        yield
        xo = x_ref[rows, :] + gate_ref[...] * (w1 * y1 + w2 * y2)
        o_ref[rows, :] = _rms(xo) * fn_ref[...]

    _round_robin([stages(sub) for sub in range(n_sub)])


def _moe_tables(meta, n_steps, gran_per_step, gran_per_tile):
    i32 = jnp.int32
    cnt = meta[:, 0, :N_EXPERTS].astype(i32)
    seg = meta[:, 1, :N_EXPERTS].astype(i32)
    n_tiles = cnt.shape[0]
    earlier = jnp.cumsum(cnt, axis=0) - cnt
    steps = (jnp.sum(cnt, axis=0) + gran_per_step - 1) // gran_per_step
    step_end = jnp.cumsum(steps)
    used = step_end[-1:]
    seg_pos = (step_end - steps)[None, :] * gran_per_step + earlier
    seg_src = jnp.arange(n_tiles, dtype=i32)[:, None] * gran_per_tile + seg

    s = jnp.arange(n_steps * gran_per_step, dtype=i32)[:, None]
    pos, length, start = (a.reshape(1, -1) for a in (seg_pos, cnt, seg_src))
    src = jnp.sum(jnp.where((s >= pos) & (s < pos + length), start - pos + s, 0), axis=1)
    st = jnp.arange(n_steps, dtype=i32)[:, None]
    step_expert = jnp.minimum(jnp.sum((step_end[None, :] <= st).astype(i32), axis=1), N_EXPERTS - 1)

    g = jnp.arange(gran_per_tile, dtype=i32)[None, :, None]
    lo, n, base = (a[:, None, :] for a in (seg, cnt, seg_pos))
    inv = jnp.sum(jnp.where((g >= lo) & (g < lo + n), base + g - lo, 0), axis=-1).reshape(-1)
    return src, step_expert, used, inv


def _mixer_moe(a, w_out, x2, mod3, gain, w_router, w_gate_up, w_down, final_norm, *, layer, batch, seq, tb):
    t, d = x2.shape
    kdim = a.shape[1]
    n_e, fe, _ = w_down.shape
    tiles_per_seq = seq // tb
    n_tiles = t // tb
    slots, gran_per_tile = _moe_slots(tb)
    gran_per_step = MOE_LHS_ROWS // MOE_GRANULE
    n_steps = pl.cdiv(MOE_TOP_K * t // MOE_GRANULE + n_tiles * n_e, gran_per_step) + n_e
    wr = jnp.pad(w_router, ((0, 0), (0, LANES - n_e)))
    wr_hi = wr.astype(BF16)
    wr_lo = (wr - wr_hi.astype(F32)).astype(BF16)
    wr2 = jnp.stack([wr_hi, wr_lo])
    wgu = w_gate_up.astype(BF16)
    row_spec = lambda rows, cols: pl.BlockSpec((rows, cols), lambda i, *_: (i, 0))

    per = MOE_ROUTE_TILES
    assert tiles_per_seq % per == 0
    steps_per_seq = tiles_per_seq // per
    x2, hs, rt, meta = pl.pallas_call(
        functools.partial(_moe_route_kernel, tb=tb),
        grid=(n_tiles // per,),
        in_specs=[
            row_spec(per * tb, kdim), _resident((kdim, d)),
            _mod_spec(layer, 2, batch, steps_per_seq, d),
            row_spec(per * tb, d), _resident((1, d)),
            _mod_spec(layer, 4, batch, steps_per_seq, d),
            _mod_spec(layer, 3, batch, steps_per_seq, d),
            _resident((2, d, LANES)),
        ],
        out_specs=[row_spec(per * tb, d), row_spec(per * slots, d), row_spec(per * tb, LANES),
                   pl.BlockSpec((per, 8, LANES), lambda i: (i, 0, 0))],
        out_shape=[
            jax.ShapeDtypeStruct((t, d), F32),
            jax.ShapeDtypeStruct((n_tiles * slots, d), BF16),
            jax.ShapeDtypeStruct((t, LANES), F32),
            jax.ShapeDtypeStruct((n_tiles, 8, LANES), F32),
        ],
        compiler_params=_params("parallel"),
        name="moe_route",
    )(a, w_out.astype(BF16), mod3, x2, gain.reshape(1, d), mod3, mod3, wr2)

    src, step_expert, used, inv = _moe_tables(meta, n_steps, gran_per_step, gran_per_tile)

    def granule_spec(j, per_step):
        return pl.BlockSpec((MOE_GRANULE, d), lambda i, tbl, *_: (tbl[i * per_step + j], 0))

    ye = pl.pallas_call(
        _moe_expert_kernel,
        grid_spec=pltpu.PrefetchScalarGridSpec(
            num_scalar_prefetch=3,
            grid=(n_steps,),
            in_specs=[granule_spec(j, gran_per_step) for j in range(gran_per_step)] + [
                pl.BlockSpec((1, d, fe), lambda i, src, te, used: (te[i], 0, 0)),
                pl.BlockSpec((1, d, fe), lambda i, src, te, used: (te[i], 0, 1)),
                pl.BlockSpec((1, fe, d), lambda i, src, te, used: (te[i], 0, 0)),
            ],
            out_specs=pl.BlockSpec((MOE_LHS_ROWS, d), lambda i, *_: (i, 0)),
        ),
        out_shape=jax.ShapeDtypeStruct((n_steps * MOE_LHS_ROWS, d), BF16),
        compiler_params=_params("arbitrary"),
        name="moe_experts",
    )(src, step_expert, used, *([hs] * gran_per_step), wgu, wgu, w_down.astype(BF16))

    return pl.pallas_call(
        functools.partial(_moe_combine_kernel, tb=tb),
        grid_spec=pltpu.PrefetchScalarGridSpec(
            num_scalar_prefetch=1,
            grid=(n_tiles // per,),
            in_specs=[granule_spec(j, per * gran_per_tile) for j in range(per * gran_per_tile)] + [
                row_spec(per * tb, LANES), row_spec(per * tb, d),
                _mod_spec(layer, 5, batch, steps_per_seq, d),
                pl.BlockSpec((1, d), lambda i, *_: (0, 0)),
            ],
            out_specs=row_spec(per * tb, d),
        ),
        out_shape=jax.ShapeDtypeStruct((t, d), F32),
        compiler_params=_params("parallel"),
        name="moe_combine",
    )(inv, *([ye] * (per * gran_per_tile)), rt, x2, mod3, final_norm.reshape(1, d))


def kernel(x, c, positions, ada_w, ada_b, norm_mix, norm_ffn, mla_w_in, mla_q_norm, mla_w_qb, mla_kv_norm, mla_w_kvb, mla_w_out, ffn_w_gate_up, ffn_w_down, gdn_w_in, gdn_conv_w, gdn_a_log, gdn_dt_bias, gdn_out_norm, gdn_w_out, moe_w_router, moe_w_gate_up, moe_w_down, final_norm):
    batch, seq, d = x.shape
    depth = ada_w.shape[0]
    assert depth == 2 and seq % 512 == 0
    t = batch * seq
    tm = 512
    dims = dict(batch=batch, seq=seq, tm=tm)

    mod = _adaln(c, ada_w, ada_b)
    mod3 = mod.reshape(depth * batch * N_MOD, 1, d)
    x2 = x.reshape(t, d)
    pos = positions.astype(F32).reshape(t, 1)

    tq = 256
    q, k, vt = _mla_proj(x2, pos, mod3, norm_mix[0], mla_w_in[0], mla_q_norm[0], mla_w_qb[0],
                         mla_kv_norm[0], mla_w_kvb[0], layer=0, tkv=tq, **dims)
    attn = _mla_attention(q, k, vt, batch=batch, seq=seq, tq=tq, n_heads=16)
    x2, gq, gk, gv, gz, bg = _ffn_gdn(attn, mla_w_out[0], x2, mod3, norm_ffn[0], ffn_w_gate_up[0],
                                      ffn_w_down[0], norm_mix[1], gdn_w_in[0], gdn_conv_w[0],
                                      gdn_a_log[0], gdn_dt_bias[0], **dims)
    og = _gdn_core(gq, gk, gv, gz, bg, gdn_out_norm[0], batch=batch, seq=seq, ts=256)
    out = _mixer_moe(og, gdn_w_out[0], x2, mod3, norm_ffn[1], moe_w_router[0], moe_w_gate_up[0],
                     moe_w_down[0], final_norm, layer=1, batch=batch, seq=seq, tb=256)
    return out.reshape(batch, seq, d)
```
